```python
import jax, jax.numpy as jnp
from jax import lax
import numpy as np

D_MODEL = 4096
BATCH = 2
SEQ = 4096
DEPTH = 1

D_MIX = D_MODEL
ATT_HEADS = 16
HEAD_DIM = 128
D_ATT = ATT_HEADS * HEAD_DIM
IDX_HEADS = 32
IDX_DIM = 64
TOPK_MAX = 256
GMLP_GROUPS = 16
GMLP_CH = 128
D_GMLP = GMLP_GROUPS * GMLP_CH
CHUNK = 128
D_FF = -(-8 * D_MODEL // (3 * 256)) * 256
QBLOCK = 64
EPS = 1e-6
NEG = -1e30

SPLIT_SIZES = (D_ATT, D_ATT, D_ATT, IDX_HEADS * IDX_DIM, IDX_DIM, IDX_HEADS, D_GMLP, D_GMLP)
D_IN = sum(SPLIT_SIZES)

kernel_name = "hymba_dsa_gmlp_hybrid_block"


def rms_norm(x, g):
    xf = x.astype(jnp.float32)
    y = xf * lax.rsqrt(jnp.mean(xf * xf, axis=-1, keepdims=True) + EPS)
    return (y * g.astype(jnp.float32)).astype(x.dtype)


def dsa_sparse_attention(q, k, v, q_idx, k_idx, w_idx):
    B, L = q.shape[0], q.shape[1]
    topk = min(TOPK_MAX, L // 4)
    nblk = L // QBLOCK
    key_pos = jnp.arange(L)
    q_pos = jnp.arange(L).reshape(nblk, QBLOCK)
    k_idx_f = k_idx.astype(jnp.float32)

    def to_blocks(a):
        return a.reshape((B, nblk, QBLOCK) + a.shape[2:]).swapaxes(0, 1)

    def one_block(args):
        qb, qib, wb, pos = args
        logits = jnp.einsum('bqhd,bsd->bqhs', qib.astype(jnp.float32), k_idx_f) * (IDX_DIM ** -0.5)
        score = jnp.einsum('bqh,bqhs->bqs', wb.astype(jnp.float32), jax.nn.relu(logits))
        causal = key_pos[None, :] <= pos[:, None]
        score = jnp.where(causal[None], score, NEG)
        _, sel = lax.top_k(score, topk)
        valid = sel <= pos[None, :, None]
        k_sel = jax.vmap(lambda kk, idx: kk[idx])(k, sel)
        v_sel = jax.vmap(lambda vv, idx: vv[idx])(v, sel)
        s = jnp.einsum('bqhd,bqkhd->bqhk', qb.astype(jnp.float32), k_sel.astype(jnp.float32)) * (HEAD_DIM ** -0.5)
        s = jnp.where(valid[:, :, None, :], s, NEG)
        p = jax.nn.softmax(s, axis=-1)
        o = jnp.einsum('bqhk,bqkhd->bqhd', p.astype(v.dtype), v_sel)
        return o.reshape(B, QBLOCK, D_ATT)

    out = lax.map(one_block, (to_blocks(q), to_blocks(q_idx), to_blocks(w_idx), q_pos))
    return out.swapaxes(0, 1).reshape(B, L, D_ATT)


def chunked_spatial_gating(u, v, v_gain, w_s, b_s):
    B, L = u.shape[0], u.shape[1]
    u = jax.nn.gelu(u)
    v = jax.nn.gelu(v).reshape(B, L, GMLP_GROUPS, GMLP_CH)
    vf = v.astype(jnp.float32)
    mu = jnp.mean(vf, axis=-1, keepdims=True)
    var = jnp.mean(jnp.square(vf - mu), axis=-1, keepdims=True)
    v = ((vf - mu) * lax.rsqrt(var + EPS) * v_gain.astype(jnp.float32)).astype(u.dtype)
    v = v.reshape(B, L // CHUNK, CHUNK, GMLP_GROUPS, GMLP_CH)
    causal = jnp.tril(jnp.ones((CHUNK, CHUNK), dtype=w_s.dtype))
    w = w_s * causal[None]
    z = jnp.einsum('gts,bnsgc->bntgc', w, v) + b_s.T[None, None, :, :, None]
    return u * z.reshape(B, L, D_GMLP)


def swiglu(h, w_gate, w_up, w_down):
    return (jax.nn.silu(h @ w_gate) * (h @ w_up)) @ w_down


def setup_inputs(seed: int = 0) -> dict:
    key = jax.random.key(seed)
    ks = jax.random.split(key, 14)
    f32 = jnp.float32
    n = lambda k, shape, scale: jax.random.normal(k, shape, f32) * scale
    return {
        "x": jax.random.normal(ks[0], (BATCH, SEQ, D_MODEL), f32),
        "norm_mix": 1.0 + n(ks[1], (DEPTH, D_MODEL), 0.02),
        "w_in": n(ks[2], (DEPTH, D_MODEL, D_IN), D_MODEL ** -0.5),
        "gmlp_v_gain": 1.0 + n(ks[3], (DEPTH, GMLP_GROUPS, GMLP_CH), 0.02),
        "w_spatial": n(ks[4], (DEPTH, GMLP_GROUPS, CHUNK, CHUNK), CHUNK ** -0.5),
        "b_spatial": 1.0 + n(ks[5], (DEPTH, GMLP_GROUPS, CHUNK), 0.02),
        "w_out": n(ks[6], (DEPTH, D_MIX, D_MODEL), D_MIX ** -0.5),
        "norm_ffn": 1.0 + n(ks[7], (DEPTH, D_MODEL), 0.02),
        "w_gate": n(ks[8], (DEPTH, D_MODEL, D_FF), D_MODEL ** -0.5),
        "w_up": n(ks[9], (DEPTH, D_MODEL, D_FF), D_MODEL ** -0.5),
        "w_down": n(ks[10], (DEPTH, D_FF, D_MODEL), D_FF ** -0.5),
        "norm_final": 1.0 + n(ks[11], (D_MODEL,), 0.02),
    }


def reference(x, norm_mix, w_in, gmlp_v_gain, w_spatial, b_spatial, w_out,
              norm_ffn, w_gate, w_up, w_down, norm_final):
    B, L = x.shape[0], x.shape[1]
    offsets = list(np.cumsum(SPLIT_SIZES)[:-1])
    for i in range(DEPTH):
        h = rms_norm(x, norm_mix[i])
        proj = h @ w_in[i]
        q, k, v, q_idx, k_idx, w_idx, u_g, v_g = jnp.split(proj, offsets, axis=-1)
        q = q.reshape(B, L, ATT_HEADS, HEAD_DIM)
        k = k.reshape(B, L, ATT_HEADS, HEAD_DIM)
        v = v.reshape(B, L, ATT_HEADS, HEAD_DIM)
        q_idx = q_idx.reshape(B, L, IDX_HEADS, IDX_DIM)
        w_idx = w_idx * (IDX_HEADS ** -0.5)
        att = dsa_sparse_attention(q, k, v, q_idx, k_idx, w_idx)
        gm = chunked_spatial_gating(u_g, v_g, gmlp_v_gain[i], w_spatial[i], b_spatial[i])
        mix = jnp.concatenate([att, gm], axis=-1)
        x = x + mix @ w_out[i]
        h = rms_norm(x, norm_ffn[i])
        x = x + swiglu(h, w_gate[i], w_up[i], w_down[i])
    return rms_norm(x, norm_final)
```

```python
import functools

import jax
import jax.numpy as jnp
from jax import lax
from jax.experimental import pallas as pl
from jax.experimental.pallas import tpu as pltpu

F32 = jnp.float32
BF16 = jnp.bfloat16

ATT_HEADS = 16
HEAD_DIM = 128
IDX_HEADS = 32
IDX_DIM = 64
TOPK_MAX = 256
GMLP_GROUPS = 16
GMLP_CH = 128
CHUNK = 128
EPS = 1e-6
NEG = -1e30
BIG = 1e30

LANES = 128
VMEM_LIMIT = 56 * 1024 * 1024
BISECT_ITERS = 32


def _cparams(sem):
    return pltpu.CompilerParams(dimension_semantics=sem, vmem_limit_bytes=VMEM_LIMIT)


def _rmsnorm_kernel(x_ref, g_ref, o_ref):
    x = x_ref[...].astype(F32)
    ms = jnp.mean(x * x, axis=-1, keepdims=True)
    o_ref[...] = (x * lax.rsqrt(ms + EPS) * g_ref[...]).astype(o_ref.dtype)


def rmsnorm(x, g, out_dtype, tm=256):
    m, d = x.shape
    tm = min(tm, m)
    return pl.pallas_call(
        _rmsnorm_kernel,
        grid=(m // tm,),
        in_specs=[pl.BlockSpec((tm, d), lambda i: (i, 0)),
                  pl.BlockSpec((1, d), lambda i: (0, 0))],
        out_specs=pl.BlockSpec((tm, d), lambda i: (i, 0)),
        out_shape=jax.ShapeDtypeStruct((m, d), out_dtype),
        compiler_params=_cparams(("parallel",)),
        name="rmsnorm",
    )(x, g.reshape(1, d).astype(F32))


def _matmul_kernel(*refs, n_pairs, has_res):
    a_refs = refs[:n_pairs]
    b_refs = refs[n_pairs:2 * n_pairs]
    o_ref = refs[-1]
    acc = jnp.dot(a_refs[0][...], b_refs[0][...], preferred_element_type=F32)
    for p in range(1, n_pairs):
        acc = acc + jnp.dot(a_refs[p][...], b_refs[p][...], preferred_element_type=F32)
    if has_res:
        acc = acc + refs[2 * n_pairs][...]
    o_ref[...] = acc.astype(o_ref.dtype)


def matmul(a_list, b_list, out_dtype, res=None, tm=1024, tn=1024, name="matmul"):
    n_pairs = len(a_list)
    m = a_list[0][0].shape[0]
    n = b_list[0].shape[1]
    tm = min(tm, m)
    tn = min(tn, n)
    in_specs, args = [], []
    for (a, cb), b in zip(a_list, b_list):
        k = b.shape[0]
        in_specs.append(pl.BlockSpec((tm, k), lambda i, j, cb=cb: (i, cb)))
        args.append(a)
    for b in b_list:
        k = b.shape[0]
        in_specs.append(pl.BlockSpec((k, tn), lambda i, j: (0, j)))
        args.append(b)
    if res is not None:
        in_specs.append(pl.BlockSpec((tm, tn), lambda i, j: (i, j)))
        args.append(res)
    return pl.pallas_call(
        functools.partial(_matmul_kernel, n_pairs=n_pairs, has_res=res is not None),
        grid=(m // tm, n // tn),
        in_specs=in_specs,
        out_specs=pl.BlockSpec((tm, tn), lambda i, j: (i, j)),
        out_shape=jax.ShapeDtypeStruct((m, n), out_dtype),
        compiler_params=_cparams(("parallel", "parallel")),
        name=name,
    )(*args)


def _swiglu_up_kernel(h_ref, wg_ref, wu_ref, o_ref):
    h = h_ref[...]
    g = jnp.dot(h, wg_ref[...], preferred_element_type=F32)
    u = jnp.dot(h, wu_ref[...], preferred_element_type=F32)
    o_ref[...] = (g * jax.nn.sigmoid(g) * u).astype(o_ref.dtype)


def swiglu_up(h, wg, wu, tm=1024, tn=256):
    m, k = h.shape
    n = wg.shape[1]
    tm = min(tm, m)
    tn = min(tn, n)
    return pl.pallas_call(
        _swiglu_up_kernel,
        grid=(m // tm, n // tn),
        in_specs=[pl.BlockSpec((tm, k), lambda i, j: (i, 0)),
                  pl.BlockSpec((k, tn), lambda i, j: (0, j)),
                  pl.BlockSpec((k, tn), lambda i, j: (0, j))],
        out_specs=pl.BlockSpec((tm, tn), lambda i, j: (i, j)),
        out_shape=jax.ShapeDtypeStruct((m, n), BF16),
        compiler_params=_cparams(("parallel", "parallel")),
        name="swiglu_up",
    )(h, wg, wu)


def _matmul_k_kernel(a_ref, b_ref, r_ref, o_ref, acc_ref):
    kk = pl.program_id(2)

    @pl.when(kk == 0)
    def _():
        acc_ref[...] = r_ref[...]

    acc_ref[...] += jnp.dot(a_ref[...], b_ref[...], preferred_element_type=F32)

    @pl.when(kk == pl.num_programs(2) - 1)
    def _():
        o_ref[...] = acc_ref[...].astype(o_ref.dtype)


def matmul_k(a, b, res, out_dtype, tm=1024, tn=512, tk=5504, name="matmul_k"):
    m, k = a.shape
    n = b.shape[1]
    tm = min(tm, m)
    tn = min(tn, n)
    tk = min(tk, k)
    return pl.pallas_call(
        _matmul_k_kernel,
        grid=(m // tm, n // tn, k // tk),
        in_specs=[pl.BlockSpec((tm, tk), lambda i, j, kk: (i, kk)),
                  pl.BlockSpec((tk, tn), lambda i, j, kk: (kk, j)),
                  pl.BlockSpec((tm, tn), lambda i, j, kk: (i, j))],
        out_specs=pl.BlockSpec((tm, tn), lambda i, j, kk: (i, j)),
        out_shape=jax.ShapeDtypeStruct((m, n), out_dtype),
        scratch_shapes=[pltpu.VMEM((tm, tn), F32)],
        compiler_params=_cparams(("parallel", "parallel", "arbitrary")),
        name=name,
    )(a, b, res)


def _select_kernel(q_ref, ke_ref, ko_ref, w_ref, bias_ref, s_ref, *, tq, tk, n_chunks, topk):
    i = pl.program_id(1)
    t0 = i * tq
    nj = (t0 + tq + tk - 1) // tk
    rows = t0 + lax.broadcasted_iota(jnp.int32, (tq, 1), 0)
    w = w_ref[0]

    def score_chunk(j, carry):
        rmin, rmax = carry
        ke = ke_ref[0, j]
        ko = ko_ref[0, j]
        acc = jnp.zeros((tq, tk), F32)
        for p in range(IDX_HEADS // 2):
            qp = q_ref[0, p]
            le = jnp.dot(qp, ke, preferred_element_type=F32)
            lo = jnp.dot(qp, ko, preferred_element_type=F32)
            acc = (acc + w[:, 2 * p:2 * p + 1] * jnp.maximum(le, 0.0)
                   + w[:, 2 * p + 1:2 * p + 2] * jnp.maximum(lo, 0.0))
        cols = j * tk + lax.broadcasted_iota(jnp.int32, (1, tk), 1)
        causal = cols <= rows
        s_ref[j] = jnp.where(causal, acc, NEG)
        rmax = jnp.maximum(rmax, jnp.max(jnp.where(causal, acc, NEG), axis=1, keepdims=True))
        rmin = jnp.minimum(rmin, jnp.min(jnp.where(causal, acc, BIG), axis=1, keepdims=True))
        return rmin, rmax

    rmin, rmax = lax.fori_loop(
        0, nj, score_chunk,
        (jnp.full((tq, 1), BIG, F32), jnp.full((tq, 1), NEG, F32)))

    keff = jnp.minimum(rows + 1, topk).astype(F32)

    def count_ge(thr):
        def body(j, c):
            part = jnp.where(s_ref[j] >= thr, 1.0, 0.0)
            for cc in range(tk // LANES):
                c = c + part[:, cc * LANES:(cc + 1) * LANES]
            return c
        c = lax.fori_loop(0, nj, body, jnp.zeros((tq, LANES), F32))
        return jnp.sum(c, axis=1, keepdims=True)

    at_max = count_ge(rmax) >= keff
    lo0 = jnp.where(at_max, rmax, rmin)

    def bisect(_, lh):
        lo, hi = lh
        mid = 0.5 * (lo + hi)
        ge = count_ge(mid) >= keff
        return jnp.where(ge, mid, lo), jnp.where(ge, hi, mid)

    lo, _ = lax.fori_loop(0, BISECT_ITERS, bisect, (lo0, rmax))

    def write_sel(j, c):
        bias_ref[0, j] = jnp.where(s_ref[j] >= lo, 0.0, NEG).astype(bias_ref.dtype)
        return c

    def write_none(j, c):
        bias_ref[0, j] = jnp.full((tq, tk), NEG, bias_ref.dtype)
        return c

    lax.fori_loop(0, nj, write_sel, 0)
    lax.fori_loop(nj, n_chunks, write_none, 0)


def select_mask(qx, ke, ko, w, *, tq, tk, topk):
    bsz, npair, seq, _ = qx.shape
    n_chunks = seq // tk
    return pl.pallas_call(
        functools.partial(_select_kernel, tq=tq, tk=tk, n_chunks=n_chunks, topk=topk),
        grid=(bsz, seq // tq),
        in_specs=[pl.BlockSpec((1, npair, tq, LANES), lambda b, i: (b, 0, i, 0)),
                  pl.BlockSpec((1, n_chunks, LANES, tk), lambda b, i: (b, 0, 0, 0)),
                  pl.BlockSpec((1, n_chunks, LANES, tk), lambda b, i: (b, 0, 0, 0)),
                  pl.BlockSpec((1, tq, IDX_HEADS), lambda b, i: (b, i, 0))],
        out_specs=pl.BlockSpec((1, n_chunks, tq, tk), lambda b, i: (b, 0, i, 0)),
        out_shape=jax.ShapeDtypeStruct((bsz, n_chunks, seq, tk), BF16),
        scratch_shapes=[pltpu.VMEM((n_chunks, tq, tk), F32)],
        compiler_params=_cparams(("parallel", "parallel")),
        name="select_mask",
    )(qx, ke, ko, w)


def _attn_kernel(q_ref, k_ref, v_ref, b_ref, o_ref, m_ref, l_ref, acc_ref, *, tq, tk, nh, dh):
    i = pl.program_id(1)
    j = pl.program_id(2)

    @pl.when(j == 0)
    def _():
        m_ref[...] = jnp.full(m_ref.shape, -jnp.inf, F32)
        l_ref[...] = jnp.zeros(l_ref.shape, F32)
        acc_ref[...] = jnp.zeros(acc_ref.shape, F32)

    @pl.when(j * tk <= i * tq + tq - 1)
    def _():
        bias = b_ref[0, 0].astype(F32)
        for h in range(nh):
            hs = slice(h * dh, (h + 1) * dh)
            s = lax.dot_general(q_ref[:, hs], k_ref[:, hs], (((1,), (1,)), ((), ())),
                                preferred_element_type=F32) + bias
            m_old = m_ref[h]
            m_new = jnp.maximum(m_old, jnp.max(s, axis=1, keepdims=True))
            p = jnp.exp(s - m_new[:, :1])
            alpha = jnp.exp(m_old - m_new)
            l_ref[h] = alpha * l_ref[h] + jnp.sum(p, axis=1, keepdims=True)
            acc_ref[:, hs] = alpha * acc_ref[:, hs] + jnp.dot(
                p.astype(v_ref.dtype), v_ref[:, hs], preferred_element_type=F32)
            m_ref[h] = m_new

    @pl.when(j == pl.num_programs(2) - 1)
    def _():
        for h in range(nh):
            hs = slice(h * dh, (h + 1) * dh)
            o_ref[:, hs] = (acc_ref[:, hs] / l_ref[h]).astype(o_ref.dtype)


def masked_attention(qkv, bias, *, bsz, seq, tq, tk):
    nh, dh = ATT_HEADS, HEAD_DIM
    d_att = nh * dh
    nqb, nkb = seq // tq, seq // tk

    def jc(i, j):
        return jnp.minimum(j, (i * tq + tq - 1) // tk)

    return pl.pallas_call(
        functools.partial(_attn_kernel, tq=tq, tk=tk, nh=nh, dh=dh),
        grid=(bsz, nqb, nkb),
        in_specs=[pl.BlockSpec((tq, d_att), lambda b, i, j: (b * nqb + i, 0)),
                  pl.BlockSpec((tk, d_att), lambda b, i, j: (b * nkb + jc(i, j), 1)),
                  pl.BlockSpec((tk, d_att), lambda b, i, j: (b * nkb + jc(i, j), 2)),
                  pl.BlockSpec((1, 1, tq, tk), lambda b, i, j: (b, jc(i, j), i, 0))],
        out_specs=pl.BlockSpec((tq, d_att), lambda b, i, j: (b * nqb + i, 0)),
        out_shape=jax.ShapeDtypeStruct((bsz * seq, d_att), BF16),
        scratch_shapes=[pltpu.VMEM((nh, tq, LANES), F32),
                        pltpu.VMEM((nh, tq, LANES), F32),
                        pltpu.VMEM((tq, d_att), F32)],
        compiler_params=_cparams(("parallel", "parallel", "arbitrary")),
        name="masked_attention",
    )(qkv, qkv, qkv, bias)


def _gelu_tanh(x):
    return 0.5 * x * (1.0 + jnp.tanh(0.7978845608028654 * (x + 0.044715 * (x * x * x))))


def _gmlp_kernel(u_ref, v_ref, gain_ref, w_ref, bias_ref, o_ref, *, n_sub):
    tri = (lax.broadcasted_iota(jnp.int32, (CHUNK, CHUNK), 1)
           <= lax.broadcasted_iota(jnp.int32, (CHUNK, CHUNK), 0))
    for g in range(GMLP_GROUPS):
        gs = slice(g * GMLP_CH, (g + 1) * GMLP_CH)
        wg = jnp.where(tri, w_ref[g], 0.0).astype(BF16)
        gain = gain_ref[:, gs]
        bias = bias_ref[:, gs]
        for c in range(n_sub):
            rs = slice(c * CHUNK, (c + 1) * CHUNK)
            u = _gelu_tanh(u_ref[rs, gs].astype(F32))
            v = _gelu_tanh(v_ref[rs, gs].astype(F32))
            mu = jnp.mean(v, axis=-1, keepdims=True)
            vc = v - mu
            var = jnp.mean(vc * vc, axis=-1, keepdims=True)
            vn = (vc * lax.rsqrt(var + EPS) * gain).astype(BF16)
            z = jnp.dot(wg, vn, preferred_element_type=F32) + bias
            o_ref[rs, gs] = (u * z).astype(o_ref.dtype)


def gmlp_gating(uv, v_gain, w_s, b_s, tm=512):
    m = uv.shape[0]
    d = GMLP_GROUPS * GMLP_CH
    tm = min(tm, m)
    gain = v_gain.reshape(1, d).astype(F32)
    bias_full = jnp.repeat(b_s.T.astype(F32), GMLP_CH, axis=1)
    return pl.pallas_call(
        functools.partial(_gmlp_kernel, n_sub=tm // CHUNK),
        grid=(m // tm,),
        in_specs=[pl.BlockSpec((tm, d), lambda i: (i, 0)),
                  pl.BlockSpec((tm, d), lambda i: (i, 1)),
                  pl.BlockSpec((1, d), lambda i: (0, 0)),
                  pl.BlockSpec((GMLP_GROUPS, CHUNK, CHUNK), lambda i: (0, 0, 0)),
                  pl.BlockSpec((CHUNK, d), lambda i: (0, 0))],
        out_specs=pl.BlockSpec((tm, d), lambda i: (i, 0)),
        out_shape=jax.ShapeDtypeStruct((m, d), BF16),
        compiler_params=_cparams(("parallel",)),
        name="gmlp_gating",
    )(uv, uv, gain, w_s.astype(F32), bias_full)


def dsa_attention(proj_main, proj_small, *, bsz, seq, tq=256, tk=512):
    tq = min(tq, seq)
    tk = min(tk, seq)
    topk = min(TOPK_MAX, seq // 4)
    d_att = ATT_HEADS * HEAD_DIM
    npair = IDX_HEADS // 2
    n_chunks = seq // tk
    qx = proj_main[:, 3 * d_att:].reshape(bsz, seq, npair, LANES).transpose(0, 2, 1, 3)
    kt = proj_small[:, :IDX_DIM].astype(BF16).reshape(bsz, n_chunks, tk, IDX_DIM).transpose(0, 1, 3, 2)
    zeros = jnp.zeros_like(kt)
    ke = jnp.concatenate([kt, zeros], axis=2)
    ko = jnp.concatenate([zeros, kt], axis=2)
    w = proj_small[:, IDX_DIM:IDX_DIM + IDX_HEADS].reshape(bsz, seq, IDX_HEADS)
    bias = select_mask(qx, ke, ko, w, tq=tq, tk=tk, topk=topk)
    return masked_attention(proj_main, bias, bsz=bsz, seq=seq, tq=tq, tk=tk)


def kernel(x, norm_mix, w_in, gmlp_v_gain, w_spatial, b_spatial, w_out, norm_ffn,
           w_gate, w_up, w_down, norm_final):
    bsz, seq, d_model = x.shape
    m = bsz * seq
    d_att = ATT_HEADS * HEAD_DIM
    d_gm = GMLP_GROUPS * GMLP_CH
    n_idx = IDX_HEADS * IDX_DIM
    depth = w_in.shape[0]
    xf = x.reshape(m, d_model)
    for i in range(depth):
        wi = w_in[i]
        o = 0
        wq = wi[:, o:o + d_att] * (HEAD_DIM ** -0.5); o += d_att
        wkv = wi[:, o:o + 2 * d_att]; o += 2 * d_att
        wqi = wi[:, o:o + n_idx] * (IDX_DIM ** -0.5); o += n_idx
        wki = wi[:, o:o + IDX_DIM]; o += IDX_DIM
        wwi = wi[:, o:o + IDX_HEADS] * (IDX_HEADS ** -0.5); o += IDX_HEADS
        wuv = wi[:, o:o + 2 * d_gm]
        w_main = jnp.concatenate([wq, wkv, wqi], axis=1).astype(BF16)
        w_small = jnp.concatenate(
            [wki, wwi, jnp.zeros((d_model, LANES - IDX_DIM - IDX_HEADS), F32)], axis=1).astype(BF16)
        w_uv = wuv.astype(BF16)

        h = rmsnorm(xf, norm_mix[i], BF16)
        proj_main = matmul([(h, 0)], [w_main], BF16, name="proj_main")
        proj_small = matmul([(h, 0)], [w_small], F32, tn=LANES, name="proj_small")
        proj_uv = matmul([(h, 0)], [w_uv], BF16, name="proj_uv")

        att = dsa_attention(proj_main, proj_small, bsz=bsz, seq=seq)
        gm = gmlp_gating(proj_uv, gmlp_v_gain[i], w_spatial[i], b_spatial[i])

        wo = w_out[i].astype(BF16)
        xf = matmul([(att, 0), (gm, 0)], [wo[:d_att], wo[d_att:]], F32, res=xf, name="out_proj")

        h2 = rmsnorm(xf, norm_ffn[i], BF16)
        hid = swiglu_up(h2, w_gate[i].astype(BF16), w_up[i].astype(BF16))
        xf = matmul_k(hid, w_down[i].astype(BF16), xf, F32, name="ffn_down")
    out = rmsnorm(xf, norm_final, x.dtype)
    return out.reshape(bsz, seq, d_model)
```

```python
import functools

import jax
import jax.numpy as jnp
from jax import lax
from jax.experimental import pallas as pl
from jax.experimental.pallas import tpu as pltpu

F32 = jnp.float32
BF16 = jnp.bfloat16

ATT_HEADS = 16
HEAD_DIM = 128
IDX_HEADS = 32
IDX_DIM = 64
TOPK_MAX = 256
GMLP_GROUPS = 16
GMLP_CH = 128
CHUNK = 128
EPS = 1e-6
NEG = -1e30
BIG = 1e30
LOG2E = 1.4426950408889634

LANES = 128
VMEM_LIMIT = 56 * 1024 * 1024
BISECT_ITERS = 32


def _cparams(sem):
    return pltpu.CompilerParams(dimension_semantics=sem, vmem_limit_bytes=VMEM_LIMIT)


def _rmsnorm_kernel(x_ref, g_ref, o_ref):
    x = x_ref[...].astype(F32)
    ms = jnp.mean(x * x, axis=-1, keepdims=True)
    o_ref[...] = (x * lax.rsqrt(ms + EPS) * g_ref[...]).astype(o_ref.dtype)


def rmsnorm(x, g, out_dtype, tm=256):
    m, d = x.shape
    tm = min(tm, m)
    return pl.pallas_call(
        _rmsnorm_kernel,
        grid=(m // tm,),
        in_specs=[pl.BlockSpec((tm, d), lambda i: (i, 0)),
                  pl.BlockSpec((1, d), lambda i: (0, 0))],
        out_specs=pl.BlockSpec((tm, d), lambda i: (i, 0)),
        out_shape=jax.ShapeDtypeStruct((m, d), out_dtype),
        compiler_params=_cparams(("parallel",)),
        name="rmsnorm",
    )(x, g.reshape(1, d).astype(F32))


def _matmul_kernel(*refs, n_pairs, has_res):
    a_refs = refs[:n_pairs]
    b_refs = refs[n_pairs:2 * n_pairs]
    o_ref = refs[-1]
    acc = jnp.dot(a_refs[0][...], b_refs[0][...], preferred_element_type=F32)
    for p in range(1, n_pairs):
        acc = acc + jnp.dot(a_refs[p][...], b_refs[p][...], preferred_element_type=F32)
    if has_res:
        acc = acc + refs[2 * n_pairs][...]
    o_ref[...] = acc.astype(o_ref.dtype)


def matmul(a_list, b_list, out_dtype, res=None, tm=1024, tn=1024, name="matmul"):
    n_pairs = len(a_list)
    m = a_list[0][0].shape[0]
    n = b_list[0].shape[1]
    tm = min(tm, m)
    tn = min(tn, n)
    in_specs, args = [], []
    for (a, cb), b in zip(a_list, b_list):
        k = b.shape[0]
        in_specs.append(pl.BlockSpec((tm, k), lambda i, j, cb=cb: (i, cb)))
        args.append(a)
    for b in b_list:
        k = b.shape[0]
        in_specs.append(pl.BlockSpec((k, tn), lambda i, j: (0, j)))
        args.append(b)
    if res is not None:
        in_specs.append(pl.BlockSpec((tm, tn), lambda i, j: (i, j)))
        args.append(res)
    return pl.pallas_call(
        functools.partial(_matmul_kernel, n_pairs=n_pairs, has_res=res is not None),
        grid=(m // tm, n // tn),
        in_specs=in_specs,
        out_specs=pl.BlockSpec((tm, tn), lambda i, j: (i, j)),
        out_shape=jax.ShapeDtypeStruct((m, n), out_dtype),
        compiler_params=_cparams(("parallel", "parallel")),
        name=name,
    )(*args)


def _swiglu_up_kernel(h_ref, wg_ref, wu_ref, o_ref):
    h = h_ref[...]
    g = jnp.dot(h, wg_ref[...], preferred_element_type=F32)
    u = jnp.dot(h, wu_ref[...], preferred_element_type=F32)
    o_ref[...] = (g * jax.nn.sigmoid(g) * u).astype(o_ref.dtype)


def swiglu_up(h, wg, wu, tm=1024, tn=256):
    m, k = h.shape
    n = wg.shape[1]
    tm = min(tm, m)
    tn = min(tn, n)
    return pl.pallas_call(
        _swiglu_up_kernel,
        grid=(m // tm, n // tn),
        in_specs=[pl.BlockSpec((tm, k), lambda i, j: (i, 0)),
                  pl.BlockSpec((k, tn), lambda i, j: (0, j)),
                  pl.BlockSpec((k, tn), lambda i, j: (0, j))],
        out_specs=pl.BlockSpec((tm, tn), lambda i, j: (i, j)),
        out_shape=jax.ShapeDtypeStruct((m, n), BF16),
        compiler_params=_cparams(("parallel", "parallel")),
        name="swiglu_up",
    )(h, wg, wu)


def _matmul_k_kernel(a_ref, b_ref, r_ref, o_ref, acc_ref):
    kk = pl.program_id(2)

    @pl.when(kk == 0)
    def _():
        acc_ref[...] = r_ref[...]

    acc_ref[...] += jnp.dot(a_ref[...], b_ref[...], preferred_element_type=F32)

    @pl.when(kk == pl.num_programs(2) - 1)
    def _():
        o_ref[...] = acc_ref[...].astype(o_ref.dtype)


def matmul_k(a, b, res, out_dtype, tm=1024, tn=512, tk=5504, name="matmul_k"):
    m, k = a.shape
    n = b.shape[1]
    tm = min(tm, m)
    tn = min(tn, n)
    tk = min(tk, k)
    return pl.pallas_call(
        _matmul_k_kernel,
        grid=(m // tm, n // tn, k // tk),
        in_specs=[pl.BlockSpec((tm, tk), lambda i, j, kk: (i, kk)),
                  pl.BlockSpec((tk, tn), lambda i, j, kk: (kk, j)),
                  pl.BlockSpec((tm, tn), lambda i, j, kk: (i, j))],
        out_specs=pl.BlockSpec((tm, tn), lambda i, j, kk: (i, j)),
        out_shape=jax.ShapeDtypeStruct((m, n), out_dtype),
        scratch_shapes=[pltpu.VMEM((tm, tn), F32)],
        compiler_params=_cparams(("parallel", "parallel", "arbitrary")),
        name=name,
    )(a, b, res)


def _select_kernel(q_ref, ke_ref, ko_ref, w_ref, bias_ref, s_ref, *, tq, tk, n_chunks, topk):
    i = pl.program_id(1)
    t0 = i * tq
    nj = (t0 + tq + tk - 1) // tk
    rows = t0 + lax.broadcasted_iota(jnp.int32, (tq, 1), 0)
    w = w_ref[0]

    def score_chunk(j, carry):
        rmin, rmax = carry
        ke = ke_ref[0, j]
        ko = ko_ref[0, j]
        acc = jnp.zeros((tq, tk), F32)
        for p in range(IDX_HEADS // 2):
            qp = q_ref[0, p]
            le = jnp.dot(qp, ke, preferred_element_type=F32)
            lo = jnp.dot(qp, ko, preferred_element_type=F32)
            acc = (acc + w[:, 2 * p:2 * p + 1] * jnp.maximum(le, 0.0)
                   + w[:, 2 * p + 1:2 * p + 2] * jnp.maximum(lo, 0.0))
        cols = j * tk + lax.broadcasted_iota(jnp.int32, (1, tk), 1)
        causal = cols <= rows
        s_ref[j] = jnp.where(causal, acc, NEG)
        rmax = jnp.maximum(rmax, jnp.max(jnp.where(causal, acc, NEG), axis=1, keepdims=True))
        rmin = jnp.minimum(rmin, jnp.min(jnp.where(causal, acc, BIG), axis=1, keepdims=True))
        return rmin, rmax

    rmin, rmax = lax.fori_loop(
        0, nj, score_chunk,
        (jnp.full((tq, 1), BIG, F32), jnp.full((tq, 1), NEG, F32)))

    keff = jnp.minimum(rows + 1, topk).astype(F32)

    def count_ge(thr):
        def body(j, c):
            part = jnp.where(s_ref[j] >= thr, 1.0, 0.0)
            for cc in range(tk // LANES):
                c = c + part[:, cc * LANES:(cc + 1) * LANES]
            return c
        c = lax.fori_loop(0, nj, body, jnp.zeros((tq, LANES), F32))
        return jnp.sum(c, axis=1, keepdims=True)

    at_max = count_ge(rmax) >= keff
    lo0 = jnp.where(at_max, rmax, rmin)

    def bisect(_, lh):
        lo, hi = lh
        mid = 0.5 * (lo + hi)
        ge = count_ge(mid) >= keff
        return jnp.where(ge, mid, lo), jnp.where(ge, hi, mid)

    lo, _ = lax.fori_loop(0, BISECT_ITERS, bisect, (lo0, rmax))

    def write_sel(j, c):
        bias_ref[0, j] = jnp.where(s_ref[j] >= lo, 0.0, NEG).astype(bias_ref.dtype)
        return c

    def write_none(j, c):
        bias_ref[0, j] = jnp.full((tq, tk), NEG, bias_ref.dtype)
        return c

    lax.fori_loop(0, nj, write_sel, 0)
    lax.fori_loop(nj, n_chunks, write_none, 0)


def select_mask(qx, ke, ko, w, *, tq, tk, topk):
    bsz, npair, seq, _ = qx.shape
    n_chunks = seq // tk
    return pl.pallas_call(
        functools.partial(_select_kernel, tq=tq, tk=tk, n_chunks=n_chunks, topk=topk),
        grid=(bsz, seq // tq),
        in_specs=[pl.BlockSpec((1, npair, tq, LANES), lambda b, i: (b, 0, i, 0)),
                  pl.BlockSpec((1, n_chunks, LANES, tk), lambda b, i: (b, 0, 0, 0)),
                  pl.BlockSpec((1, n_chunks, LANES, tk), lambda b, i: (b, 0, 0, 0)),
                  pl.BlockSpec((1, tq, IDX_HEADS), lambda b, i: (b, i, 0))],
        out_specs=pl.BlockSpec((1, n_chunks, tq, tk), lambda b, i: (b, 0, i, 0)),
        out_shape=jax.ShapeDtypeStruct((bsz, n_chunks, seq, tk), BF16),
        scratch_shapes=[pltpu.VMEM((n_chunks, tq, tk), F32)],
        compiler_params=_cparams(("parallel", "parallel")),
        name="select_mask",
    )(qx, ke, ko, w)


def _attn_kernel(q_ref, k_ref, v_ref, b_ref, o_ref, m_ref, acc_ref, vx_ref, *, tq, tk, nh, dh):
    i = pl.program_id(1)
    j = pl.program_id(2)

    @pl.when(j == 0)
    def _():
        m_ref[...] = jnp.full(m_ref.shape, -jnp.inf, F32)
        acc_ref[...] = jnp.zeros(acc_ref.shape, F32)

    @pl.when(j * tk <= i * tq + tq - 1)
    def _():
        bias = b_ref[0, 0]
        for h in range(nh):
            vx_ref[:, 2 * h * dh:(2 * h + 1) * dh] = v_ref[:, h * dh:(h + 1) * dh]
            vx_ref[:, (2 * h + 1) * dh:(2 * h + 2) * dh] = jnp.ones((tk, dh), vx_ref.dtype)
        for h in range(nh):
            hs = slice(h * dh, (h + 1) * dh)
            xs = slice(2 * h * dh, (2 * h + 2) * dh)
            s = lax.dot_general(q_ref[:, hs], k_ref[:, hs], (((1,), (1,)), ((), ())),
                                preferred_element_type=F32)
            sb = s.astype(BF16) + bias
            part = sb[:, :LANES]
            for c in range(1, tk // LANES):
                part = jnp.maximum(part, sb[:, c * LANES:(c + 1) * LANES])
            m_old = m_ref[h]
            m_new = jnp.maximum(m_old, jnp.max(part.astype(F32), axis=1, keepdims=True))
            p = jnp.exp2(sb - m_new[:, :1].astype(BF16))
            alpha = jnp.exp2(m_old - m_new)
            pv = jnp.dot(p, vx_ref[:, xs], preferred_element_type=F32)
            acc_ref[:, xs] = jnp.concatenate([alpha, alpha], axis=1) * acc_ref[:, xs] + pv
            m_ref[h] = m_new

    @pl.when(j == pl.num_programs(2) - 1)
    def _():
        for h in range(nh):
            num = acc_ref[:, 2 * h * dh:(2 * h + 1) * dh]
            den = acc_ref[:, (2 * h + 1) * dh:(2 * h + 2) * dh]
            o_ref[:, h * dh:(h + 1) * dh] = (num / den).astype(o_ref.dtype)


def masked_attention(qkv, bias, *, bsz, seq, tq, tk):
    nh, dh = ATT_HEADS, HEAD_DIM
    d_att = nh * dh
    nqb, nkb = seq // tq, seq // tk

    def jc(i, j):
        return jnp.minimum(j, (i * tq + tq - 1) // tk)

    return pl.pallas_call(
        functools.partial(_attn_kernel, tq=tq, tk=tk, nh=nh, dh=dh),
        grid=(bsz, nqb, nkb),
        in_specs=[pl.BlockSpec((tq, d_att), lambda b, i, j: (b * nqb + i, 0)),
                  pl.BlockSpec((tk, d_att), lambda b, i, j: (b * nkb + jc(i, j), 1)),
                  pl.BlockSpec((tk, d_att), lambda b, i, j: (b * nkb + jc(i, j), 2)),
                  pl.BlockSpec((1, 1, tq, tk), lambda b, i, j: (b, jc(i, j), i, 0))],
        out_specs=pl.BlockSpec((tq, d_att), lambda b, i, j: (b * nqb + i, 0)),
        out_shape=jax.ShapeDtypeStruct((bsz * seq, d_att), BF16),
        scratch_shapes=[pltpu.VMEM((nh, tq, LANES), F32),
                        pltpu.VMEM((tq, 2 * d_att), F32),
                        pltpu.VMEM((tk, 2 * d_att), BF16)],
        compiler_params=_cparams(("parallel", "parallel", "arbitrary")),
        name="masked_attention",
    )(qkv, qkv, qkv, bias)


def _gelu_tanh(x):
    return 0.5 * x * (1.0 + jnp.tanh(0.7978845608028654 * (x + 0.044715 * (x * x * x))))


def _gmlp_kernel(u_ref, v_ref, gain_ref, w_ref, bias_ref, o_ref, *, n_sub):
    tri = (lax.broadcasted_iota(jnp.int32, (CHUNK, CHUNK), 1)
           <= lax.broadcasted_iota(jnp.int32, (CHUNK, CHUNK), 0))
    for g in range(GMLP_GROUPS):
        gs = slice(g * GMLP_CH, (g + 1) * GMLP_CH)
        wg = jnp.where(tri, w_ref[g], 0.0).astype(BF16)
        gain = gain_ref[:, gs]
        bias = bias_ref[:, gs]
        for c in range(n_sub):
            rs = slice(c * CHUNK, (c + 1) * CHUNK)
            u = _gelu_tanh(u_ref[rs, gs].astype(F32))
            v = _gelu_tanh(v_ref[rs, gs].astype(F32))
            mu = jnp.mean(v, axis=-1, keepdims=True)
            vc = v - mu
            var = jnp.mean(vc * vc, axis=-1, keepdims=True)
            vn = (vc * lax.rsqrt(var + EPS) * gain).astype(BF16)
            z = jnp.dot(wg, vn, preferred_element_type=F32) + bias
            o_ref[rs, gs] = (u * z).astype(o_ref.dtype)


def gmlp_gating(uv, v_gain, w_s, b_s, tm=512):
    m = uv.shape[0]
    d = GMLP_GROUPS * GMLP_CH
    tm = min(tm, m)
    gain = v_gain.reshape(1, d).astype(F32)
    bias_full = jnp.repeat(b_s.T.astype(F32), GMLP_CH, axis=1)
    return pl.pallas_call(
        functools.partial(_gmlp_kernel, n_sub=tm // CHUNK),
        grid=(m // tm,),
        in_specs=[pl.BlockSpec((tm, d), lambda i: (i, 0)),
                  pl.BlockSpec((tm, d), lambda i: (i, 1)),
                  pl.BlockSpec((1, d), lambda i: (0, 0)),
                  pl.BlockSpec((GMLP_GROUPS, CHUNK, CHUNK), lambda i: (0, 0, 0)),
                  pl.BlockSpec((CHUNK, d), lambda i: (0, 0))],
        out_specs=pl.BlockSpec((tm, d), lambda i: (i, 0)),
        out_shape=jax.ShapeDtypeStruct((m, d), BF16),
        compiler_params=_cparams(("parallel",)),
        name="gmlp_gating",
    )(uv, uv, gain, w_s.astype(F32), bias_full)


def dsa_attention(proj_main, proj_small, *, bsz, seq, tq=256, tk=512):
    tq = min(tq, seq)
    tk = min(tk, seq)
    topk = min(TOPK_MAX, seq // 4)
    d_att = ATT_HEADS * HEAD_DIM
    npair = IDX_HEADS // 2
    n_chunks = seq // tk
    qx = proj_main[:, 3 * d_att:].reshape(bsz, seq, npair, LANES).transpose(0, 2, 1, 3)
    kt = proj_small[:, :IDX_DIM].astype(BF16).reshape(bsz, n_chunks, tk, IDX_DIM).transpose(0, 1, 3, 2)
    zeros = jnp.zeros_like(kt)
    ke = jnp.concatenate([kt, zeros], axis=2)
    ko = jnp.concatenate([zeros, kt], axis=2)
    w = proj_small[:, IDX_DIM:IDX_DIM + IDX_HEADS].reshape(bsz, seq, IDX_HEADS)
    bias = select_mask(qx, ke, ko, w, tq=tq, tk=tk, topk=topk)
    return masked_attention(proj_main, bias, bsz=bsz, seq=seq, tq=tq, tk=tk)


def kernel(x, norm_mix, w_in, gmlp_v_gain, w_spatial, b_spatial, w_out, norm_ffn,
           w_gate, w_up, w_down, norm_final):
    bsz, seq, d_model = x.shape
    m = bsz * seq
    d_att = ATT_HEADS * HEAD_DIM
    d_gm = GMLP_GROUPS * GMLP_CH
    n_idx = IDX_HEADS * IDX_DIM
    depth = w_in.shape[0]
    xf = x.reshape(m, d_model)
    for i in range(depth):
        wi = w_in[i]
        o = 0
        wq = wi[:, o:o + d_att] * (LOG2E * HEAD_DIM ** -0.5); o += d_att
        wkv = wi[:, o:o + 2 * d_att]; o += 2 * d_att
        wqi = wi[:, o:o + n_idx] * (IDX_DIM ** -0.5); o += n_idx
        wki = wi[:, o:o + IDX_DIM]; o += IDX_DIM
        wwi = wi[:, o:o + IDX_HEADS] * (IDX_HEADS ** -0.5); o += IDX_HEADS
        wuv = wi[:, o:o + 2 * d_gm]
        w_main = jnp.concatenate([wq, wkv, wqi], axis=1).astype(BF16)
        w_small = jnp.concatenate(
            [wki, wwi, jnp.zeros((d_model, LANES - IDX_DIM - IDX_HEADS), F32)], axis=1).astype(BF16)
        w_uv = wuv.astype(BF16)

        h = rmsnorm(xf, norm_mix[i], BF16)
        proj_main = matmul([(h, 0)], [w_main], BF16, name="proj_main")
        proj_small = matmul([(h, 0)], [w_small], F32, tn=LANES, name="proj_small")
        proj_uv = matmul([(h, 0)], [w_uv], BF16, name="proj_uv")

        att = dsa_attention(proj_main, proj_small, bsz=bsz, seq=seq)
        gm = gmlp_gating(proj_uv, gmlp_v_gain[i], w_spatial[i], b_spatial[i])

        wo = w_out[i].astype(BF16)
        xf = matmul([(att, 0), (gm, 0)], [wo[:d_att], wo[d_att:]], F32, res=xf, name="out_proj")

        h2 = rmsnorm(xf, norm_ffn[i], BF16)
        hid = swiglu_up(h2, w_gate[i].astype(BF16), w_up[i].astype(BF16))
        xf = matmul_k(hid, w_down[i].astype(BF16), xf, F32, name="ffn_down")
    out = rmsnorm(xf, norm_final, x.dtype)
    return out.reshape(bsz, seq, d_model)
```

```python
import functools

import jax
import jax.numpy as jnp
from jax import lax
from jax.experimental import pallas as pl
from jax.experimental.pallas import tpu as pltpu

F32 = jnp.float32
BF16 = jnp.bfloat16

ATT_HEADS = 16
HEAD_DIM = 128
IDX_HEADS = 32
IDX_DIM = 64
TOPK_MAX = 256
GMLP_GROUPS = 16
GMLP_CH = 128
CHUNK = 128
EPS = 1e-6
NEG = -1e30
BIG = 1e30
LOG2E = 1.4426950408889634

LANES = 128
BF16_SUBLANES = 16
VMEM_LIMIT = 56 * 1024 * 1024
BISECT_MAX_ITERS = 48


def _cparams(sem):
    return pltpu.CompilerParams(dimension_semantics=sem, vmem_limit_bytes=VMEM_LIMIT)


def _rmsnorm_kernel(x_ref, g_ref, o_ref):
    x = x_ref[...].astype(F32)
    ms = jnp.mean(x * x, axis=-1, keepdims=True)
    o_ref[...] = (x * lax.rsqrt(ms + EPS) * g_ref[...]).astype(o_ref.dtype)


def rmsnorm(x, g, out_dtype, tm=256):
    m, d = x.shape
    tm = min(tm, m)
    return pl.pallas_call(
        _rmsnorm_kernel,
        grid=(m // tm,),
        in_specs=[pl.BlockSpec((tm, d), lambda i: (i, 0)),
                  pl.BlockSpec((1, d), lambda i: (0, 0))],
        out_specs=pl.BlockSpec((tm, d), lambda i: (i, 0)),
        out_shape=jax.ShapeDtypeStruct((m, d), out_dtype),
        compiler_params=_cparams(("parallel",)),
        name="rmsnorm",
    )(x, g.reshape(1, d).astype(F32))


def _matmul_kernel(*refs, n_pairs, has_scale, has_res, cast_w, out_mode):
    a_refs = refs[:n_pairs]
    w_refs = refs[n_pairs:2 * n_pairs]
    pos = 2 * n_pairs
    sc_ref = refs[pos] if has_scale else None
    pos += int(has_scale)
    r_ref = refs[pos] if has_res else None
    pos += int(has_res)
    o_ref = refs[pos]
    wb_refs = refs[pos + 1:]

    if cast_w:
        @pl.when(pl.program_id(1) == 0)
        def _():
            for p in range(n_pairs):
                w = w_refs[p][...]
                if has_scale:
                    w = w * sc_ref[...]
                wb_refs[p][...] = w.astype(BF16)
        rhs = wb_refs
    else:
        rhs = w_refs

    acc = jnp.dot(a_refs[0][...], rhs[0][...], preferred_element_type=F32)
    for p in range(1, n_pairs):
        acc = acc + jnp.dot(a_refs[p][...], rhs[p][...], preferred_element_type=F32)
    if has_res:
        acc = acc + r_ref[...]

    tn = acc.shape[1]
    if out_mode == "plain":
        o_ref[...] = acc.astype(o_ref.dtype)
    elif out_mode == "ones":
        for hh in range(tn // LANES):
            o_ref[:, 2 * hh * LANES:(2 * hh + 1) * LANES] = (
                acc[:, hh * LANES:(hh + 1) * LANES].astype(o_ref.dtype))
            o_ref[:, (2 * hh + 1) * LANES:(2 * hh + 2) * LANES] = jnp.ones(
                (acc.shape[0], LANES), o_ref.dtype)
    else:
        for pp in range(tn // LANES):
            o_ref[0, pp] = acc[:, pp * LANES:(pp + 1) * LANES].astype(o_ref.dtype)


def matmul(a_list, w_list, out_dtype, *, n, col_off=0, scale=None, res=None,
           out_mode="plain", seq=None, tm=1024, tn=512, name="matmul"):
    n_pairs = len(a_list)
    m = a_list[0].shape[0]
    tm = min(tm, m)
    tn = min(tn, n)
    cast_w = w_list[0][0].dtype != BF16
    in_specs, args, scratch = [], [], []
    for a in a_list:
        in_specs.append(pl.BlockSpec((tm, a.shape[1]), lambda j, i: (i, 0)))
        args.append(a)
    for a, (w, rb) in zip(a_list, w_list):
        in_specs.append(pl.BlockSpec((a.shape[1], tn), lambda j, i, rb=rb: (rb, j + col_off)))
        args.append(w)
        if cast_w:
            scratch.append(pltpu.VMEM((a.shape[1], tn), BF16))
    if scale is not None:
        in_specs.append(pl.BlockSpec((1, tn), lambda j, i: (0, j + col_off)))
        args.append(scale)
    if res is not None:
        in_specs.append(pl.BlockSpec((tm, tn), lambda j, i: (i, j)))
        args.append(res)
    if out_mode == "plain":
        out_spec = pl.BlockSpec((tm, tn), lambda j, i: (i, j))
        out_shape = jax.ShapeDtypeStruct((m, n), out_dtype)
    elif out_mode == "ones":
        out_spec = pl.BlockSpec((tm, 2 * tn), lambda j, i: (i, j))
        out_shape = jax.ShapeDtypeStruct((m, 2 * n), out_dtype)
    else:
        nmb = seq // tm
        out_spec = pl.BlockSpec((1, tn // LANES, tm, LANES), lambda j, i: (i // nmb, j, i % nmb, 0))
        out_shape = jax.ShapeDtypeStruct((m // seq, n // LANES, seq, LANES), out_dtype)
    return pl.pallas_call(
        functools.partial(_matmul_kernel, n_pairs=n_pairs, has_scale=scale is not None,
                          has_res=res is not None, cast_w=cast_w, out_mode=out_mode),
        grid=(n // tn, m // tm),
        in_specs=in_specs,
        out_specs=out_spec,
        out_shape=out_shape,
        scratch_shapes=scratch,
        compiler_params=_cparams(("parallel", "arbitrary")),
        name=name,
    )(*args)


def _swiglu_up_kernel(h_ref, wg_ref, wu_ref, wd_ref, o_ref, wdb_ref, wgb_ref, wub_ref):
    @pl.when(pl.program_id(1) == 0)
    def _():
        wgb_ref[...] = wg_ref[...].astype(BF16)
        wub_ref[...] = wu_ref[...].astype(BF16)

    wdb_ref[...] = wd_ref[...].astype(BF16)
    h = h_ref[...]
    g = jnp.dot(h, wgb_ref[...], preferred_element_type=F32)
    u = jnp.dot(h, wub_ref[...], preferred_element_type=F32)
    o_ref[...] = (g * jax.nn.sigmoid(g) * u).astype(o_ref.dtype)


def swiglu_up(h, wg, wu, wd, tm=1024, tn=256):
    m, k = h.shape
    n = wg.shape[1]
    d = wd.shape[1]
    tm = min(tm, m)
    tn = min(tn, n)
    nmb = m // tm
    steps = (n // tn) * nmb
    slab = n // steps
    assert slab * steps == n and slab % BF16_SUBLANES == 0, (n, steps)
    return pl.pallas_call(
        _swiglu_up_kernel,
        grid=(n // tn, nmb),
        in_specs=[pl.BlockSpec((tm, k), lambda j, i: (i, 0)),
                  pl.BlockSpec((k, tn), lambda j, i: (0, j)),
                  pl.BlockSpec((k, tn), lambda j, i: (0, j)),
                  pl.BlockSpec((slab, d), lambda j, i: (j * nmb + i, 0))],
        out_specs=[pl.BlockSpec((tm, tn), lambda j, i: (i, j)),
                   pl.BlockSpec((slab, d), lambda j, i: (j * nmb + i, 0))],
        out_shape=[jax.ShapeDtypeStruct((m, n), BF16),
                   jax.ShapeDtypeStruct((n, d), BF16)],
        scratch_shapes=[pltpu.VMEM((k, tn), BF16), pltpu.VMEM((k, tn), BF16)],
        compiler_params=_cparams(("parallel", "arbitrary")),
        name="swiglu_up",
    )(h, wg, wu, wd)


def _matmul_k_kernel(a_ref, b_ref, r_ref, o_ref, acc_ref):
    kk = pl.program_id(2)

    @pl.when(kk == 0)
    def _():
        acc_ref[...] = r_ref[...]

    acc_ref[...] += jnp.dot(a_ref[...], b_ref[...], preferred_element_type=F32)

    @pl.when(kk == pl.num_programs(2) - 1)
    def _():
        o_ref[...] = acc_ref[...].astype(o_ref.dtype)


def matmul_k(a, b, res, out_dtype, tm=1024, tn=512, tk=5504, name="matmul_k"):
    m, k = a.shape
    n = b.shape[1]
    tm = min(tm, m)
    tn = min(tn, n)
    tk = min(tk, k)
    return pl.pallas_call(
        _matmul_k_kernel,
        grid=(m // tm, n // tn, k // tk),
        in_specs=[pl.BlockSpec((tm, tk), lambda i, j, kk: (i, kk)),
                  pl.BlockSpec((tk, tn), lambda i, j, kk: (kk, j)),
                  pl.BlockSpec((tm, tn), lambda i, j, kk: (i, j))],
        out_specs=pl.BlockSpec((tm, tn), lambda i, j, kk: (i, j)),
        out_shape=jax.ShapeDtypeStruct((m, n), out_dtype),
        scratch_shapes=[pltpu.VMEM((tm, tn), F32)],
        compiler_params=_cparams(("parallel", "parallel", "arbitrary")),
        name=name,
    )(a, b, res)


def _select_kernel(q_ref, ke_ref, ko_ref, w_ref, bias_ref, s_ref, *, tq, tk, n_chunks, topk):
    i = pl.program_id(1)
    t0 = i * tq
    nj = (t0 + tq + tk - 1) // tk
    rows = t0 + lax.broadcasted_iota(jnp.int32, (tq, 1), 0)
    w = w_ref[0]
    nlt = tk // LANES

    def score_chunk(j, carry):
        rmin, rmax = carry
        ke = ke_ref[0, j]
        ko = ko_ref[0, j]
        acc = jnp.zeros((tq, tk), F32)
        for p in range(IDX_HEADS // 2):
            qp = q_ref[0, p]
            le = jnp.dot(qp, ke, preferred_element_type=F32)
            lo = jnp.dot(qp, ko, preferred_element_type=F32)
            acc = (acc + w[:, 2 * p:2 * p + 1] * jnp.maximum(le, 0.0)
                   + w[:, 2 * p + 1:2 * p + 2] * jnp.maximum(lo, 0.0))
        cols = j * tk + lax.broadcasted_iota(jnp.int32, (1, tk), 1)
        causal = cols <= rows
        s_ref[j] = jnp.where(causal, acc, NEG)
        rmax = jnp.maximum(rmax, jnp.max(jnp.where(causal, acc, NEG), axis=1, keepdims=True))
        rmin = jnp.minimum(rmin, jnp.min(jnp.where(causal, acc, BIG), axis=1, keepdims=True))
        return rmin, rmax

    rmin, rmax = lax.fori_loop(
        0, nj, score_chunk,
        (jnp.full((tq, 1), BIG, F32), jnp.full((tq, 1), NEG, F32)))

    keff = jnp.minimum(rows + 1, topk).astype(F32)

    def count_ge(thr):
        def body(j, c):
            part = jnp.where(s_ref[j] >= thr, 1.0, 0.0)
            for cc in range(nlt):
                c = c + part[:, cc * LANES:(cc + 1) * LANES]
            return c
        c = lax.fori_loop(0, nj, body, jnp.zeros((tq, LANES), F32))
        return jnp.sum(c, axis=1, keepdims=True)

    hi0 = rmax
    c_max = count_ge(hi0)
    at_max = c_max >= keff
    lo0 = jnp.where(at_max, hi0, rmin)
    c_lo0 = jnp.where(at_max, c_max, (rows + 1).astype(F32))

    def cond(st):
        it, _, _, c_lo = st
        return jnp.logical_and(it < BISECT_MAX_ITERS, jnp.max(jnp.abs(c_lo - keff)) > 0.0)

    def bisect(st):
        it, lo, hi, c_lo = st
        mid = 0.5 * (lo + hi)
        c = count_ge(mid)
        ge = c >= keff
        return (it + 1, jnp.where(ge, mid, lo), jnp.where(ge, hi, mid), jnp.where(ge, c, c_lo))

    _, lo, _, _ = lax.while_loop(cond, bisect, (jnp.int32(0), lo0, hi0, c_lo0))

    def write_sel(j, c):
        bias_ref[0, j] = jnp.where(s_ref[j] >= lo, 0.0, NEG).astype(bias_ref.dtype)
        return c

    def write_none(j, c):
        bias_ref[0, j] = jnp.full((tq, tk), NEG, bias_ref.dtype)
        return c

    lax.fori_loop(0, nj, write_sel, 0)
    lax.fori_loop(nj, n_chunks, write_none, 0)


def select_mask(qx, ke, ko, w, *, tq, tk, topk):
    bsz, npair, seq, _ = qx.shape
    n_chunks = seq // tk
    return pl.pallas_call(
        functools.partial(_select_kernel, tq=tq, tk=tk, n_chunks=n_chunks, topk=topk),
        grid=(bsz, seq // tq),
        in_specs=[pl.BlockSpec((1, npair, tq, LANES), lambda b, i: (b, 0, i, 0)),
                  pl.BlockSpec((1, n_chunks, LANES, tk), lambda b, i: (b, 0, 0, 0)),
                  pl.BlockSpec((1, n_chunks, LANES, tk), lambda b, i: (b, 0, 0, 0)),
                  pl.BlockSpec((1, tq, IDX_HEADS), lambda b, i: (b, i, 0))],
        out_specs=pl.BlockSpec((1, n_chunks, tq, tk), lambda b, i: (b, 0, i, 0)),
        out_shape=jax.ShapeDtypeStruct((bsz, n_chunks, seq, tk), BF16),
        scratch_shapes=[pltpu.VMEM((n_chunks, tq, tk), F32)],
        compiler_params=_cparams(("parallel", "parallel")),
        name="select_mask",
    )(qx, ke, ko, w)


def _attn_kernel(q_ref, k_ref, vx_ref, b_ref, o_ref, m_ref, acc_ref, *, tq, tk, nh, dh):
    i = pl.program_id(1)
    j = pl.program_id(2)

    @pl.when(j == 0)
    def _():
        m_ref[...] = jnp.full(m_ref.shape, -jnp.inf, F32)
        acc_ref[...] = jnp.zeros(acc_ref.shape, F32)

    @pl.when(j * tk <= i * tq + tq - 1)
    def _():
        bias = b_ref[0, 0]
        for h in range(nh):
            hs = slice(h * dh, (h + 1) * dh)
            xs = slice(2 * h * dh, (2 * h + 2) * dh)
            s = lax.dot_general(q_ref[:, hs], k_ref[:, hs], (((1,), (1,)), ((), ())),
                                preferred_element_type=F32)
            sb = s.astype(BF16) + bias
            part = sb[:, :LANES]
            for c in range(1, tk // LANES):
                part = jnp.maximum(part, sb[:, c * LANES:(c + 1) * LANES])
            m_old = m_ref[h]
            m_new = jnp.maximum(m_old, jnp.max(part.astype(F32), axis=1, keepdims=True))
            p = jnp.exp2(sb - m_new[:, :1].astype(BF16))
            alpha = jnp.exp2(m_old - m_new)
            pv = jnp.dot(p, vx_ref[:, xs], preferred_element_type=F32)
            acc_ref[:, xs] = jnp.concatenate([alpha, alpha], axis=1) * acc_ref[:, xs] + pv
            m_ref[h] = m_new

    @pl.when(j == pl.num_programs(2) - 1)
    def _():
        for h in range(nh):
            num = acc_ref[:, 2 * h * dh:(2 * h + 1) * dh]
            den = acc_ref[:, (2 * h + 1) * dh:(2 * h + 2) * dh]
            o_ref[:, h * dh:(h + 1) * dh] = (num / den).astype(o_ref.dtype)


def masked_attention(qk, vx, bias, *, bsz, seq, tq, tk):
    nh, dh = ATT_HEADS, HEAD_DIM
    d_att = nh * dh
    nqb, nkb = seq // tq, seq // tk

    def jc(i, j):
        return jnp.minimum(j, (i * tq + tq - 1) // tk)

    return pl.pallas_call(
        functools.partial(_attn_kernel, tq=tq, tk=tk, nh=nh, dh=dh),
        grid=(bsz, nqb, nkb),
        in_specs=[pl.BlockSpec((tq, d_att), lambda b, i, j: (b * nqb + i, 0)),
                  pl.BlockSpec((tk, d_att), lambda b, i, j: (b * nkb + jc(i, j), 1)),
                  pl.BlockSpec((tk, 2 * d_att), lambda b, i, j: (b * nkb + jc(i, j), 0)),
                  pl.BlockSpec((1, 1, tq, tk), lambda b, i, j: (b, jc(i, j), i, 0))],
        out_specs=pl.BlockSpec((tq, d_att), lambda b, i, j: (b * nqb + i, 0)),
        out_shape=jax.ShapeDtypeStruct((bsz * seq, d_att), BF16),
        scratch_shapes=[pltpu.VMEM((nh, tq, LANES), F32),
                        pltpu.VMEM((tq, 2 * d_att), F32)],
        compiler_params=_cparams(("parallel", "parallel", "arbitrary")),
        name="masked_attention",
    )(qk, qk, vx, bias)


def _gelu_tanh(x):
    return 0.5 * x * (1.0 + jnp.tanh(0.7978845608028654 * (x + 0.044715 * (x * x * x))))


def _gmlp_kernel(u_ref, v_ref, gain_ref, w_ref, bias_ref, o_ref, *, n_sub):
    tri = (lax.broadcasted_iota(jnp.int32, (CHUNK, CHUNK), 1)
           <= lax.broadcasted_iota(jnp.int32, (CHUNK, CHUNK), 0))
    for g in range(GMLP_GROUPS):
        gs = slice(g * GMLP_CH, (g + 1) * GMLP_CH)
        wg = jnp.where(tri, w_ref[g], 0.0).astype(BF16)
        gain = gain_ref[:, gs]
        bias = bias_ref[:, gs]
        for c in range(n_sub):
            rs = slice(c * CHUNK, (c + 1) * CHUNK)
            u = _gelu_tanh(u_ref[rs, gs].astype(F32))
            v = _gelu_tanh(v_ref[rs, gs].astype(F32))
            mu = jnp.mean(v, axis=-1, keepdims=True)
            vc = v - mu
            var = jnp.mean(vc * vc, axis=-1, keepdims=True)
            vn = (vc * lax.rsqrt(var + EPS) * gain).astype(BF16)
            z = jnp.dot(wg, vn, preferred_element_type=F32) + bias
            o_ref[rs, gs] = (u * z).astype(o_ref.dtype)


def gmlp_gating(uv, v_gain, w_s, b_s, tm=512):
    m = uv.shape[0]
    d = GMLP_GROUPS * GMLP_CH
    tm = min(tm, m)
    gain = v_gain.reshape(1, d).astype(F32)
    bias_full = jnp.repeat(b_s.T.astype(F32), GMLP_CH, axis=1)
    return pl.pallas_call(
        functools.partial(_gmlp_kernel, n_sub=tm // CHUNK),
        grid=(m // tm,),
        in_specs=[pl.BlockSpec((tm, d), lambda i: (i, 0)),
                  pl.BlockSpec((tm, d), lambda i: (i, 1)),
                  pl.BlockSpec((1, d), lambda i: (0, 0)),
                  pl.BlockSpec((GMLP_GROUPS, CHUNK, CHUNK), lambda i: (0, 0, 0)),
                  pl.BlockSpec((CHUNK, d), lambda i: (0, 0))],
        out_specs=pl.BlockSpec((tm, d), lambda i: (i, 0)),
        out_shape=jax.ShapeDtypeStruct((m, d), BF16),
        compiler_params=_cparams(("parallel",)),
        name="gmlp_gating",
    )(uv, uv, gain, w_s.astype(F32), bias_full)


def dsa_attention(qk, vx, qx, proj_small, *, bsz, seq, tq=256, tk=512):
    tq = min(tq, seq)
    tk = min(tk, seq)
    topk = min(TOPK_MAX, seq // 4)
    n_chunks = seq // tk
    kt = proj_small[:, :IDX_DIM].astype(BF16).reshape(bsz, n_chunks, tk, IDX_DIM).transpose(0, 1, 3, 2)
    zeros = jnp.zeros_like(kt)
    ke = jnp.concatenate([kt, zeros], axis=2)
    ko = jnp.concatenate([zeros, kt], axis=2)
    w = proj_small[:, IDX_DIM:IDX_DIM + IDX_HEADS].reshape(bsz, seq, IDX_HEADS)
    bias = select_mask(qx, ke, ko, w, tq=tq, tk=tk, topk=topk)
    return masked_attention(qk, vx, bias, bsz=bsz, seq=seq, tq=tq, tk=tk)


def _in_proj_scale(d_att, n_idx, d_gm):
    return jnp.concatenate([
        jnp.full((d_att,), LOG2E * HEAD_DIM ** -0.5, F32),
        jnp.ones((2 * d_att,), F32),
        jnp.full((n_idx,), IDX_DIM ** -0.5, F32),
        jnp.ones((IDX_DIM,), F32),
        jnp.full((IDX_HEADS,), IDX_HEADS ** -0.5, F32),
        jnp.ones((2 * d_gm,), F32)]).reshape(1, -1)


def kernel(x, norm_mix, w_in, gmlp_v_gain, w_spatial, b_spatial, w_out, norm_ffn,
           w_gate, w_up, w_down, norm_final):
    bsz, seq, d_model = x.shape
    m = bsz * seq
    d_att = ATT_HEADS * HEAD_DIM
    d_gm = GMLP_GROUPS * GMLP_CH
    n_idx = IDX_HEADS * IDX_DIM
    depth = w_in.shape[0]
    tn = 512
    scale = _in_proj_scale(d_att, n_idx, d_gm)
    xf = x.reshape(m, d_model)
    for i in range(depth):
        wi = w_in[i]
        uv_off = 3 * d_att + n_idx + IDX_DIM + IDX_HEADS
        w_uv = wi[:, uv_off:].astype(BF16)

        h = rmsnorm(xf, norm_mix[i], BF16)
        qk = matmul([h], [(wi, 0)], BF16, n=2 * d_att, col_off=0, scale=scale, tn=tn, name="proj_qk")
        vx = matmul([h], [(wi, 0)], BF16, n=d_att, col_off=2 * d_att // tn, scale=scale,
                    out_mode="ones", tn=tn, name="proj_v")
        qx = matmul([h], [(wi, 0)], BF16, n=n_idx, col_off=3 * d_att // tn, scale=scale,
                    out_mode="slabs", seq=seq, tn=tn, name="proj_qidx")
        proj_small = matmul([h], [(wi, 0)], F32, n=LANES, col_off=(3 * d_att + n_idx) // LANES,
                            scale=scale, tn=LANES, name="proj_small")
        proj_uv = matmul([h], [(w_uv, 0)], BF16, n=2 * d_gm, tn=1024, name="proj_uv")

        att = dsa_attention(qk, vx, qx, proj_small, bsz=bsz, seq=seq)
        gm = gmlp_gating(proj_uv, gmlp_v_gain[i], w_spatial[i], b_spatial[i])

        xf = matmul([att, gm], [(w_out[i], 0), (w_out[i], 1)], F32, n=d_model, res=xf,
                    tn=tn, name="out_proj")

        h2 = rmsnorm(xf, norm_ffn[i], BF16)
        hid, wd = swiglu_up(h2, w_gate[i], w_up[i], w_down[i])
        xf = matmul_k(hid, wd, xf, F32, name="ffn_down")
    out = rmsnorm(xf, norm_final, x.dtype)
    return out.reshape(bsz, seq, d_model)
```

```python
import functools

import jax
import jax.numpy as jnp
from jax import lax
from jax.experimental import pallas as pl
from jax.experimental.pallas import tpu as pltpu

F32 = jnp.float32
BF16 = jnp.bfloat16

ATT_HEADS = 16
HEAD_DIM = 128
IDX_HEADS = 32
IDX_DIM = 64
TOPK_MAX = 256
GMLP_GROUPS = 16
GMLP_CH = 128
CHUNK = 128
EPS = 1e-6
NEG = -1e30
BIG = 1e30
LOG2E = 1.4426950408889634

LANES = 128
BF16_SUBLANES = 16
VMEM_LIMIT = 56 * 1024 * 1024
BISECT_MAX_ITERS = 48


def _cparams(sem):
    return pltpu.CompilerParams(dimension_semantics=sem, vmem_limit_bytes=VMEM_LIMIT)


def _rmsnorm_kernel(x_ref, g_ref, o_ref):
    x = x_ref[...].astype(F32)
    ms = jnp.mean(x * x, axis=-1, keepdims=True)
    o_ref[...] = (x * lax.rsqrt(ms + EPS) * g_ref[...]).astype(o_ref.dtype)


def rmsnorm(x, g, out_dtype, tm=256):
    m, d = x.shape
    tm = min(tm, m)
    return pl.pallas_call(
        _rmsnorm_kernel,
        grid=(m // tm,),
        in_specs=[pl.BlockSpec((tm, d), lambda i: (i, 0)),
                  pl.BlockSpec((1, d), lambda i: (0, 0))],
        out_specs=pl.BlockSpec((tm, d), lambda i: (i, 0)),
        out_shape=jax.ShapeDtypeStruct((m, d), out_dtype),
        compiler_params=_cparams(("parallel",)),
        name="rmsnorm",
    )(x, g.reshape(1, d).astype(F32))


def _matmul_kernel(*refs, n_pairs, has_scale, has_res, cast_w, w_t, out_mode):
    a_refs = refs[:n_pairs]
    w_refs = refs[n_pairs:2 * n_pairs]
    pos = 2 * n_pairs
    sc_ref = refs[pos] if has_scale else None
    pos += int(has_scale)
    r_ref = refs[pos] if has_res else None
    pos += int(has_res)
    o_ref = refs[pos]
    wb_refs = refs[pos + 1:]

    if cast_w:
        @pl.when(pl.program_id(1) == 0)
        def _():
            for p in range(n_pairs):
                w = w_refs[p][...]
                if has_scale:
                    w = w * sc_ref[...]
                wb_refs[p][...] = w.astype(BF16)
        rhs = wb_refs
    else:
        rhs = w_refs

    nt = (((1,), (1,)), ((), ()))
    if out_mode == "slabs_t":
        acc = lax.dot_general(rhs[0][...], a_refs[0][...], nt, preferred_element_type=F32)
        for pp in range(acc.shape[0] // LANES):
            o_ref[0, pp] = acc[pp * LANES:(pp + 1) * LANES, :].astype(o_ref.dtype)
        return

    def mm(p):
        if w_t:
            return lax.dot_general(a_refs[p][...], rhs[p][...], nt, preferred_element_type=F32)
        return jnp.dot(a_refs[p][...], rhs[p][...], preferred_element_type=F32)

    acc = mm(0)
    for p in range(1, n_pairs):
        acc = acc + mm(p)
    if has_res:
        acc = acc + r_ref[...]

    tn = acc.shape[1]
    if out_mode == "plain":
        o_ref[...] = acc.astype(o_ref.dtype)
    else:
        for hh in range(tn // LANES):
            o_ref[:, 2 * hh * LANES:(2 * hh + 1) * LANES] = (
                acc[:, hh * LANES:(hh + 1) * LANES].astype(o_ref.dtype))
            o_ref[:, (2 * hh + 1) * LANES:(2 * hh + 2) * LANES] = jnp.ones(
                (acc.shape[0], LANES), o_ref.dtype)


def matmul(a_list, w_list, out_dtype, *, n, col_off=0, w_t=False, scale=None, res=None,
           out_mode="plain", seq=None, tm=1024, tn=512, name="matmul"):
    n_pairs = len(a_list)
    m = a_list[0].shape[0]
    tm = min(tm, m)
    tn = min(tn, n)
    cast_w = w_list[0][0].dtype != BF16
    in_specs, args, scratch = [], [], []
    for a in a_list:
        in_specs.append(pl.BlockSpec((tm, a.shape[1]), lambda j, i: (i, 0)))
        args.append(a)
    for a, (w, rb) in zip(a_list, w_list):
        k = a.shape[1]
        if w_t:
            in_specs.append(pl.BlockSpec((tn, k), lambda j, i: (j + col_off, 0)))
        else:
            in_specs.append(pl.BlockSpec((k, tn), lambda j, i, rb=rb: (rb, j + col_off)))
        args.append(w)
        if cast_w:
            scratch.append(pltpu.VMEM((tn, k) if w_t else (k, tn), BF16))
    if scale is not None:
        if w_t:
            in_specs.append(pl.BlockSpec((tn, 1), lambda j, i: (j + col_off, 0)))
        else:
            in_specs.append(pl.BlockSpec((1, tn), lambda j, i: (0, j + col_off)))
        args.append(scale)
    if res is not None:
        in_specs.append(pl.BlockSpec((tm, tn), lambda j, i: (i, j)))
        args.append(res)
    if out_mode == "plain":
        out_spec = pl.BlockSpec((tm, tn), lambda j, i: (i, j))
        out_shape = jax.ShapeDtypeStruct((m, n), out_dtype)
    elif out_mode == "ones":
        out_spec = pl.BlockSpec((tm, 2 * tn), lambda j, i: (i, j))
        out_shape = jax.ShapeDtypeStruct((m, 2 * n), out_dtype)
    else:
        assert w_t and n_pairs == 1 and res is None
        nmb = seq // tm
        out_spec = pl.BlockSpec((1, tn // LANES, LANES, tm), lambda j, i: (i // nmb, j, 0, i % nmb))
        out_shape = jax.ShapeDtypeStruct((m // seq, n // LANES, LANES, seq), out_dtype)
    return pl.pallas_call(
        functools.partial(_matmul_kernel, n_pairs=n_pairs, has_scale=scale is not None,
                          has_res=res is not None, cast_w=cast_w, w_t=w_t, out_mode=out_mode),
        grid=(n // tn, m // tm),
        in_specs=in_specs,
        out_specs=out_spec,
        out_shape=out_shape,
        scratch_shapes=scratch,
        compiler_params=_cparams(("parallel", "arbitrary")),
        name=name,
    )(*args)


def _swiglu_up_kernel(h_ref, wg_ref, wu_ref, wd_ref, o_ref, wdb_ref, wgb_ref, wub_ref):
    @pl.when(pl.program_id(1) == 0)
    def _():
        wgb_ref[...] = wg_ref[...].astype(BF16)
        wub_ref[...] = wu_ref[...].astype(BF16)

    wdb_ref[...] = wd_ref[...].astype(BF16)
    h = h_ref[...]
    g = jnp.dot(h, wgb_ref[...], preferred_element_type=F32)
    u = jnp.dot(h, wub_ref[...], preferred_element_type=F32)
    o_ref[...] = (g * jax.nn.sigmoid(g) * u).astype(o_ref.dtype)


def swiglu_up(h, wg, wu, wd, tm=1024, tn=256):
    m, k = h.shape
    n = wg.shape[1]
    d = wd.shape[1]
    tm = min(tm, m)
    tn = min(tn, n)
    nmb = m // tm
    steps = (n // tn) * nmb
    slab = n // steps
    assert slab * steps == n and slab % BF16_SUBLANES == 0, (n, steps)
    return pl.pallas_call(
        _swiglu_up_kernel,
        grid=(n // tn, nmb),
        in_specs=[pl.BlockSpec((tm, k), lambda j, i: (i, 0)),
                  pl.BlockSpec((k, tn), lambda j, i: (0, j)),
                  pl.BlockSpec((k, tn), lambda j, i: (0, j)),
                  pl.BlockSpec((slab, d), lambda j, i: (j * nmb + i, 0))],
        out_specs=[pl.BlockSpec((tm, tn), lambda j, i: (i, j)),
                   pl.BlockSpec((slab, d), lambda j, i: (j * nmb + i, 0))],
        out_shape=[jax.ShapeDtypeStruct((m, n), BF16),
                   jax.ShapeDtypeStruct((n, d), BF16)],
        scratch_shapes=[pltpu.VMEM((k, tn), BF16), pltpu.VMEM((k, tn), BF16)],
        compiler_params=_cparams(("parallel", "arbitrary")),
        name="swiglu_up",
    )(h, wg, wu, wd)


def _matmul_k_kernel(a_ref, b_ref, r_ref, o_ref, acc_ref):
    kk = pl.program_id(2)

    @pl.when(kk == 0)
    def _():
        acc_ref[...] = r_ref[...]

    acc_ref[...] += jnp.dot(a_ref[...], b_ref[...], preferred_element_type=F32)

    @pl.when(kk == pl.num_programs(2) - 1)
    def _():
        o_ref[...] = acc_ref[...].astype(o_ref.dtype)


def matmul_k(a, b, res, out_dtype, tm=1024, tn=512, tk=5504, name="matmul_k"):
    m, k = a.shape
    n = b.shape[1]
    tm = min(tm, m)
    tn = min(tn, n)
    tk = min(tk, k)
    return pl.pallas_call(
        _matmul_k_kernel,
        grid=(m // tm, n // tn, k // tk),
        in_specs=[pl.BlockSpec((tm, tk), lambda i, j, kk: (i, kk)),
                  pl.BlockSpec((tk, tn), lambda i, j, kk: (kk, j)),
                  pl.BlockSpec((tm, tn), lambda i, j, kk: (i, j))],
        out_specs=pl.BlockSpec((tm, tn), lambda i, j, kk: (i, j)),
        out_shape=jax.ShapeDtypeStruct((m, n), out_dtype),
        scratch_shapes=[pltpu.VMEM((tm, tn), F32)],
        compiler_params=_cparams(("parallel", "parallel", "arbitrary")),
        name=name,
    )(a, b, res)


def _select_kernel(q_ref, ke_ref, ko_ref, w_ref, bias_ref, s_ref, *, tq, tk, n_chunks, topk):
    i = pl.program_id(1)
    t0 = i * tq
    nj = (t0 + tq + tk - 1) // tk
    qpos = t0 + lax.broadcasted_iota(jnp.int32, (1, tq), 1)
    w = w_ref[0]

    def score_chunk(j, carry):
        rmin, rmax = carry
        r0 = pl.multiple_of(j * tk, tk)
        ke = ke_ref[0, pl.ds(r0, tk), :]
        ko = ko_ref[0, pl.ds(r0, tk), :]
        acc = jnp.zeros((tk, tq), F32)
        for p in range(IDX_HEADS // 2):
            qp = q_ref[0, p]
            le = jnp.dot(ke, qp, preferred_element_type=F32)
            lo = jnp.dot(ko, qp, preferred_element_type=F32)
            acc = (acc + w[2 * p:2 * p + 1, :] * jnp.maximum(le, 0.0)
                   + w[2 * p + 1:2 * p + 2, :] * jnp.maximum(lo, 0.0))
        kpos = j * tk + lax.broadcasted_iota(jnp.int32, (tk, 1), 0)
        causal = kpos <= qpos
        s_ref[j] = jnp.where(causal, acc, NEG)
        rmax = jnp.maximum(rmax, jnp.max(jnp.where(causal, acc, NEG), axis=0, keepdims=True))
        rmin = jnp.minimum(rmin, jnp.min(jnp.where(causal, acc, BIG), axis=0, keepdims=True))
        return rmin, rmax

    rmin, rmax = lax.fori_loop(
        0, nj, score_chunk,
        (jnp.full((1, tq), BIG, F32), jnp.full((1, tq), NEG, F32)))

    n_causal = (qpos + 1).astype(F32)
    keff = jnp.minimum(n_causal, float(topk))
    nacc = 64

    def count_ge(thr):
        def body(j, c):
            part = jnp.where(s_ref[j] >= thr, 1.0, 0.0)
            return c + jnp.sum(part.reshape(tk // nacc, nacc, tq), axis=0)
        c = lax.fori_loop(0, nj, body, jnp.zeros((nacc, tq), F32))
        return jnp.sum(c, axis=0, keepdims=True)

    hi0 = rmax
    c_max = count_ge(hi0)
    at_max = c_max >= keff
    lo0 = jnp.where(at_max, hi0, rmin)
    c_lo0 = jnp.where(at_max, c_max, n_causal)

    def cond(st):
        it, _, _, c_lo = st
        return jnp.logical_and(it < BISECT_MAX_ITERS, jnp.max(jnp.abs(c_lo - keff)) > 0.0)

    def halve(lo, hi, c_lo):
        mid = 0.5 * (lo + hi)
        c = count_ge(mid)
        ge = c >= keff
        return jnp.where(ge, mid, lo), jnp.where(ge, hi, mid), jnp.where(ge, c, c_lo)

    def bisect(st):
        it, lo, hi, c_lo = st
        lo, hi, c_lo = halve(*halve(lo, hi, c_lo))
        return it + 2, lo, hi, c_lo

    _, lo, _, _ = lax.while_loop(cond, bisect, (jnp.int32(0), lo0, hi0, c_lo0))

    def write_sel(j, c):
        sel_t = jnp.where(s_ref[j] >= lo, 0.0, NEG)
        bias_ref[0, j] = sel_t.T.astype(bias_ref.dtype)
        return c

    def write_none(j, c):
        bias_ref[0, j] = jnp.full((tq, tk), NEG, bias_ref.dtype)
        return c

    lax.fori_loop(0, nj, write_sel, 0)
    lax.fori_loop(nj, n_chunks, write_none, 0)


def select_mask(qt, ke, ko, wt, *, tq, tk, topk):
    bsz, npair, _, seq = qt.shape
    n_chunks = seq // tk
    return pl.pallas_call(
        functools.partial(_select_kernel, tq=tq, tk=tk, n_chunks=n_chunks, topk=topk),
        grid=(bsz, seq // tq),
        in_specs=[pl.BlockSpec((1, npair, LANES, tq), lambda b, i: (b, 0, 0, i)),
                  pl.BlockSpec((1, seq, LANES), lambda b, i: (b, 0, 0)),
                  pl.BlockSpec((1, seq, LANES), lambda b, i: (b, 0, 0)),
                  pl.BlockSpec((1, IDX_HEADS, tq), lambda b, i: (b, 0, i))],
        out_specs=pl.BlockSpec((1, n_chunks, tq, tk), lambda b, i: (b, 0, i, 0)),
        out_shape=jax.ShapeDtypeStruct((bsz, n_chunks, seq, tk), BF16),
        scratch_shapes=[pltpu.VMEM((n_chunks, tk, tq), F32)],
        compiler_params=_cparams(("parallel", "parallel")),
        name="select_mask",
    )(qt, ke, ko, wt)


def _attn_kernel(q_ref, k_ref, vx_ref, b_ref, o_ref, m_ref, acc_ref, *, tq, tk, nh, dh):
    i = pl.program_id(1)
    j = pl.program_id(2)

    @pl.when(j == 0)
    def _():
        m_ref[...] = jnp.full(m_ref.shape, -jnp.inf, F32)
        acc_ref[...] = jnp.zeros(acc_ref.shape, F32)

    @pl.when(j * tk <= i * tq + tq - 1)
    def _():
        bias = b_ref[0, 0]
        for h in range(nh):
            hs = slice(h * dh, (h + 1) * dh)
            xs = slice(2 * h * dh, (2 * h + 2) * dh)
            s = lax.dot_general(q_ref[:, hs], k_ref[:, hs], (((1,), (1,)), ((), ())),
                                preferred_element_type=F32)
            sb = s.astype(BF16) + bias
            part = sb[:, :LANES]
            for c in range(1, tk // LANES):
                part = jnp.maximum(part, sb[:, c * LANES:(c + 1) * LANES])
            m_old = m_ref[h]
            m_new = jnp.maximum(m_old, jnp.max(part.astype(F32), axis=1, keepdims=True))
            p = jnp.exp2(sb - m_new[:, :1].astype(BF16))
            alpha = jnp.exp2(m_old - m_new)
            pv = jnp.dot(p, vx_ref[:, xs], preferred_element_type=F32)
            acc_ref[:, xs] = jnp.concatenate([alpha, alpha], axis=1) * acc_ref[:, xs] + pv
            m_ref[h] = m_new

    @pl.when(j == pl.num_programs(2) - 1)
    def _():
        for h in range(nh):
            num = acc_ref[:, 2 * h * dh:(2 * h + 1) * dh]
            den = acc_ref[:, (2 * h + 1) * dh:(2 * h + 2) * dh]
            o_ref[:, h * dh:(h + 1) * dh] = (num / den).astype(o_ref.dtype)


def masked_attention(qk, vx, bias, *, bsz, seq, tq, tk):
    nh, dh = ATT_HEADS, HEAD_DIM
    d_att = nh * dh
    nqb, nkb = seq // tq, seq // tk

    def jc(i, j):
        return jnp.minimum(j, (i * tq + tq - 1) // tk)

    return pl.pallas_call(
        functools.partial(_attn_kernel, tq=tq, tk=tk, nh=nh, dh=dh),
        grid=(bsz, nqb, nkb),
        in_specs=[pl.BlockSpec((tq, d_att), lambda b, i, j: (b * nqb + i, 0)),
                  pl.BlockSpec((tk, d_att), lambda b, i, j: (b * nkb + jc(i, j), 1)),
                  pl.BlockSpec((tk, 2 * d_att), lambda b, i, j: (b * nkb + jc(i, j), 0)),
                  pl.BlockSpec((1, 1, tq, tk), lambda b, i, j: (b, jc(i, j), i, 0))],
        out_specs=pl.BlockSpec((tq, d_att), lambda b, i, j: (b * nqb + i, 0)),
        out_shape=jax.ShapeDtypeStruct((bsz * seq, d_att), BF16),
        scratch_shapes=[pltpu.VMEM((nh, tq, LANES), F32),
                        pltpu.VMEM((tq, 2 * d_att), F32)],
        compiler_params=_cparams(("parallel", "parallel", "arbitrary")),
        name="masked_attention",
    )(qk, qk, vx, bias)


def _gelu_tanh(x):
    return 0.5 * x * (1.0 + jnp.tanh(0.7978845608028654 * (x + 0.044715 * (x * x * x))))


def _gmlp_kernel(u_ref, v_ref, gain_ref, w_ref, bias_ref, o_ref, *, n_sub):
    tri = (lax.broadcasted_iota(jnp.int32, (CHUNK, CHUNK), 1)
           <= lax.broadcasted_iota(jnp.int32, (CHUNK, CHUNK), 0))
    for g in range(GMLP_GROUPS):
        gs = slice(g * GMLP_CH, (g + 1) * GMLP_CH)
        wg = jnp.where(tri, w_ref[g], 0.0).astype(BF16)
        gain = gain_ref[:, gs]
        bias = bias_ref[:, gs]
        for c in range(n_sub):
            rs = slice(c * CHUNK, (c + 1) * CHUNK)
            u = _gelu_tanh(u_ref[rs, gs].astype(F32))
            v = _gelu_tanh(v_ref[rs, gs].astype(F32))
            mu = jnp.mean(v, axis=-1, keepdims=True)
            vc = v - mu
            var = jnp.mean(vc * vc, axis=-1, keepdims=True)
            vn = (vc * lax.rsqrt(var + EPS) * gain).astype(BF16)
            z = jnp.dot(wg, vn, preferred_element_type=F32) + bias
            o_ref[rs, gs] = (u * z).astype(o_ref.dtype)


def gmlp_gating(uv, v_gain, w_s, b_s, tm=512):
    m = uv.shape[0]
    d = GMLP_GROUPS * GMLP_CH
    tm = min(tm, m)
    gain = v_gain.reshape(1, d).astype(F32)
    bias_full = jnp.repeat(b_s.T.astype(F32), GMLP_CH, axis=1)
    return pl.pallas_call(
        functools.partial(_gmlp_kernel, n_sub=tm // CHUNK),
        grid=(m // tm,),
        in_specs=[pl.BlockSpec((tm, d), lambda i: (i, 0)),
                  pl.BlockSpec((tm, d), lambda i: (i, 1)),
                  pl.BlockSpec((1, d), lambda i: (0, 0)),
                  pl.BlockSpec((GMLP_GROUPS, CHUNK, CHUNK), lambda i: (0, 0, 0)),
                  pl.BlockSpec((CHUNK, d), lambda i: (0, 0))],
        out_specs=pl.BlockSpec((tm, d), lambda i: (i, 0)),
        out_shape=jax.ShapeDtypeStruct((m, d), BF16),
        compiler_params=_cparams(("parallel",)),
        name="gmlp_gating",
    )(uv, uv, gain, w_s.astype(F32), bias_full)


def dsa_attention(qk, vx, qt, proj_small, *, bsz, seq, tq_sel=256, tq_att=512, tk=512):
    tq_sel = min(tq_sel, seq)
    tq_att = min(tq_att, seq)
    tk = min(tk, seq)
    topk = min(TOPK_MAX, seq // 4)
    kx = proj_small[:, :IDX_DIM].astype(BF16).reshape(bsz, seq, IDX_DIM)
    zeros = jnp.zeros_like(kx)
    ke = jnp.concatenate([kx, zeros], axis=2)
    ko = jnp.concatenate([zeros, kx], axis=2)
    wt = proj_small[:, IDX_DIM:IDX_DIM + IDX_HEADS].reshape(bsz, seq, IDX_HEADS).transpose(0, 2, 1)
    bias = select_mask(qt, ke, ko, wt, tq=tq_sel, tk=tk, topk=topk)
    return masked_attention(qk, vx, bias, bsz=bsz, seq=seq, tq=tq_att, tk=tk)


def _in_proj_scale(d_att, n_idx, d_gm):
    return jnp.concatenate([
        jnp.full((d_att,), LOG2E * HEAD_DIM ** -0.5, F32),
        jnp.ones((2 * d_att,), F32),
        jnp.full((n_idx,), IDX_DIM ** -0.5, F32),
        jnp.ones((IDX_DIM,), F32),
        jnp.full((IDX_HEADS,), IDX_HEADS ** -0.5, F32),
        jnp.ones((2 * d_gm,), F32)]).reshape(-1, 1)


def kernel(x, norm_mix, w_in, gmlp_v_gain, w_spatial, b_spatial, w_out, norm_ffn,
           w_gate, w_up, w_down, norm_final):
    bsz, seq, d_model = x.shape
    m = bsz * seq
    d_att = ATT_HEADS * HEAD_DIM
    d_gm = GMLP_GROUPS * GMLP_CH
    n_idx = IDX_HEADS * IDX_DIM
    depth = w_in.shape[0]
    tn = 512
    scale = _in_proj_scale(d_att, n_idx, d_gm)
    xf = x.reshape(m, d_model)
    for i in range(depth):
        wt = jnp.swapaxes(w_in[i], 0, 1)
        uv_off = 3 * d_att + n_idx + IDX_DIM + IDX_HEADS
        w_uv = wt[uv_off:].astype(BF16)

        h = rmsnorm(xf, norm_mix[i], BF16)
        qk = matmul([h], [(wt, 0)], BF16, n=2 * d_att, col_off=0, w_t=True, scale=scale,
                    tn=tn, name="proj_qk")
        vx = matmul([h], [(wt, 0)], BF16, n=d_att, col_off=2 * d_att // tn, w_t=True, scale=scale,
                    out_mode="ones", tn=tn, name="proj_v")
        qt = matmul([h], [(wt, 0)], BF16, n=n_idx, col_off=3 * d_att // tn, w_t=True, scale=scale,
                    out_mode="slabs_t", seq=seq, tn=tn, name="proj_qidx")
        proj_small = matmul([h], [(wt, 0)], F32, n=LANES, col_off=(3 * d_att + n_idx) // LANES,
                            w_t=True, scale=scale, tn=LANES, name="proj_small")
        proj_uv = matmul([h], [(w_uv, 0)], BF16, n=2 * d_gm, w_t=True, tn=1024, name="proj_uv")

        att = dsa_attention(qk, vx, qt, proj_small, bsz=bsz, seq=seq)
        gm = gmlp_gating(proj_uv, gmlp_v_gain[i], w_spatial[i], b_spatial[i])

        xf = matmul([att, gm], [(w_out[i], 0), (w_out[i], 1)], F32, n=d_model, res=xf,
                    tn=tn, name="out_proj")

        h2 = rmsnorm(xf, norm_ffn[i], BF16)
        hid, wd = swiglu_up(h2, w_gate[i], w_up[i], w_down[i])
        xf = matmul_k(hid, wd, xf, F32, name="ffn_down")
    out = rmsnorm(xf, norm_final, x.dtype)
    return out.reshape(bsz, seq, d_model)
```

```python
import functools

import jax
import jax.numpy as jnp
from jax import lax
from jax.experimental import pallas as pl
from jax.experimental.pallas import tpu as pltpu

F32 = jnp.float32
BF16 = jnp.bfloat16

ATT_HEADS = 16
HEAD_DIM = 128
IDX_HEADS = 32
IDX_DIM = 64
TOPK_MAX = 256
GMLP_GROUPS = 16
GMLP_CH = 128
CHUNK = 128
EPS = 1e-6
NEG = -1e30
BIG = 1e30
LOG2E = 1.4426950408889634

LANES = 128
BF16_SUBLANES = 16
VMEM_LIMIT = 56 * 1024 * 1024
BISECT_MAX_ITERS = 48


def _cparams(sem):
    return pltpu.CompilerParams(dimension_semantics=sem, vmem_limit_bytes=VMEM_LIMIT)


def _rmsnorm_kernel(x_ref, g_ref, o_ref):
    x = x_ref[...].astype(F32)
    ms = jnp.mean(x * x, axis=-1, keepdims=True)
    o_ref[...] = (x * lax.rsqrt(ms + EPS) * g_ref[...]).astype(o_ref.dtype)


def rmsnorm(x, g, out_dtype, tm=256):
    m, d = x.shape
    tm = min(tm, m)
    return pl.pallas_call(
        _rmsnorm_kernel,
        grid=(m // tm,),
        in_specs=[pl.BlockSpec((tm, d), lambda i: (i, 0)),
                  pl.BlockSpec((1, d), lambda i: (0, 0))],
        out_specs=pl.BlockSpec((tm, d), lambda i: (i, 0)),
        out_shape=jax.ShapeDtypeStruct((m, d), out_dtype),
        compiler_params=_cparams(("parallel",)),
        name="rmsnorm",
    )(x, g.reshape(1, d).astype(F32))


def _matmul_kernel(*refs, n_pairs, has_scale, has_res, has_side, cast_w, w_t, out_mode):
    a_refs = refs[:n_pairs]
    w_refs = refs[n_pairs:2 * n_pairs]
    pos = 2 * n_pairs
    sc_ref = refs[pos] if has_scale else None
    pos += int(has_scale)
    r_ref = refs[pos] if has_res else None
    pos += int(has_res)
    side_ref = refs[pos] if has_side else None
    pos += int(has_side)
    o_ref = refs[pos]
    pos += 1
    if has_side:
        refs[pos][...] = side_ref[...].astype(BF16)
        pos += 1
    wb_refs = refs[pos:]

    if cast_w:
        @pl.when(pl.program_id(1) == 0)
        def _():
            for p in range(n_pairs):
                w = w_refs[p][...]
                if has_scale:
                    w = w * sc_ref[...]
                wb_refs[p][...] = w.astype(BF16)
        rhs = wb_refs
    else:
        rhs = w_refs

    nt = (((1,), (1,)), ((), ()))
    if out_mode == "slabs_t":
        acc = lax.dot_general(rhs[0][...], a_refs[0][...], nt, preferred_element_type=F32)
        for pp in range(acc.shape[0] // LANES):
            o_ref[0, pp] = acc[pp * LANES:(pp + 1) * LANES, :].astype(o_ref.dtype)
        return

    def mm(p):
        if w_t:
            return lax.dot_general(a_refs[p][...], rhs[p][...], nt, preferred_element_type=F32)
        return jnp.dot(a_refs[p][...], rhs[p][...], preferred_element_type=F32)

    acc = mm(0)
    for p in range(1, n_pairs):
        acc = acc + mm(p)
    if has_res:
        acc = acc + r_ref[...]

    tn = acc.shape[1]
    if out_mode == "plain":
        o_ref[...] = acc.astype(o_ref.dtype)
    else:
        for hh in range(tn // LANES):
            o_ref[:, 2 * hh * LANES:(2 * hh + 1) * LANES] = (
                acc[:, hh * LANES:(hh + 1) * LANES].astype(o_ref.dtype))
            o_ref[:, (2 * hh + 1) * LANES:(2 * hh + 2) * LANES] = jnp.ones(
                (acc.shape[0], LANES), o_ref.dtype)


def matmul(a_list, w_list, out_dtype, *, n, col_off=0, w_t=False, scale=None, res=None,
           side=None, out_mode="plain", seq=None, tm=1024, tn=512, name="matmul"):
    n_pairs = len(a_list)
    m = a_list[0].shape[0]
    tm = min(tm, m)
    tn = min(tn, n)
    cast_w = w_list[0][0].dtype != BF16
    in_specs, args, scratch = [], [], []
    for a in a_list:
        in_specs.append(pl.BlockSpec((tm, a.shape[1]), lambda j, i: (i, 0)))
        args.append(a)
    for a, (w, rb) in zip(a_list, w_list):
        k = a.shape[1]
        if w_t:
            in_specs.append(pl.BlockSpec((tn, k), lambda j, i: (j + col_off, 0)))
        else:
            in_specs.append(pl.BlockSpec((k, tn), lambda j, i, rb=rb: (rb, j + col_off)))
        args.append(w)
        if cast_w:
            scratch.append(pltpu.VMEM((tn, k) if w_t else (k, tn), BF16))
    if scale is not None:
        if w_t:
            in_specs.append(pl.BlockSpec((tn, 1), lambda j, i: (j + col_off, 0)))
        else:
            in_specs.append(pl.BlockSpec((1, tn), lambda j, i: (0, j + col_off)))
        args.append(scale)
    if res is not None:
        in_specs.append(pl.BlockSpec((tm, tn), lambda j, i: (i, j)))
        args.append(res)
    nmb_all = m // tm
    if side is not None:
        side_spec, side_shape = _side_specs(side, (n // tn) * nmb_all, lambda j, i: j * nmb_all + i)
        in_specs.append(side_spec)
        args.append(side)
    if out_mode == "plain":
        out_spec = pl.BlockSpec((tm, tn), lambda j, i: (i, j))
        out_shape = jax.ShapeDtypeStruct((m, n), out_dtype)
    elif out_mode == "ones":
        out_spec = pl.BlockSpec((tm, 2 * tn), lambda j, i: (i, j))
        out_shape = jax.ShapeDtypeStruct((m, 2 * n), out_dtype)
    else:
        assert w_t and n_pairs == 1 and res is None
        nmb = seq // tm
        out_spec = pl.BlockSpec((1, tn // LANES, LANES, tm), lambda j, i: (i // nmb, j, 0, i % nmb))
        out_shape = jax.ShapeDtypeStruct((m // seq, n // LANES, LANES, seq), out_dtype)
    if side is not None:
        out_spec, out_shape = [out_spec, side_spec], [out_shape, side_shape]
    return pl.pallas_call(
        functools.partial(_matmul_kernel, n_pairs=n_pairs, has_scale=scale is not None,
                          has_res=res is not None, has_side=side is not None,
                          cast_w=cast_w, w_t=w_t, out_mode=out_mode),
        grid=(n // tn, m // tm),
        in_specs=in_specs,
        out_specs=out_spec,
        out_shape=out_shape,
        scratch_shapes=scratch,
        compiler_params=_cparams(("parallel", "arbitrary")),
        name=name,
    )(*args)


def _side_specs(w, steps, step_of):
    rows, cols = w.shape
    slab = rows // steps
    assert slab * steps == rows and slab % BF16_SUBLANES == 0, (rows, steps)
    spec = pl.BlockSpec((slab, cols), lambda *g: (step_of(*g), 0))
    return spec, jax.ShapeDtypeStruct((rows, cols), BF16)


def _swiglu_up_kernel(h_ref, wg_ref, wu_ref, wd_ref, o_ref, wdb_ref):
    wdb_ref[...] = wd_ref[...].astype(BF16)
    h = h_ref[...]
    g = jnp.dot(h, wg_ref[...], preferred_element_type=F32)
    u = jnp.dot(h, wu_ref[...], preferred_element_type=F32)
    o_ref[...] = (g * jax.nn.sigmoid(g) * u).astype(o_ref.dtype)


def swiglu_up(h, wg, wu, wd, tm=2048, tn=256):
    m, k = h.shape
    n = wg.shape[1]
    tm = min(tm, m)
    tn = min(tn, n)
    nnb = n // tn
    side_spec, side_shape = _side_specs(wd, (m // tm) * nnb, lambda i, j: i * nnb + j)
    return pl.pallas_call(
        _swiglu_up_kernel,
        grid=(m // tm, nnb),
        in_specs=[pl.BlockSpec((tm, k), lambda i, j: (i, 0)),
                  pl.BlockSpec((k, tn), lambda i, j: (0, j)),
                  pl.BlockSpec((k, tn), lambda i, j: (0, j)),
                  side_spec],
        out_specs=[pl.BlockSpec((tm, tn), lambda i, j: (i, j)), side_spec],
        out_shape=[jax.ShapeDtypeStruct((m, n), BF16), side_shape],
        compiler_params=_cparams(("parallel", "parallel")),
        name="swiglu_up",
    )(h, wg, wu, wd)


def _matmul_k_kernel(a_ref, b_ref, r_ref, o_ref, acc_ref):
    kk = pl.program_id(2)

    @pl.when(kk == 0)
    def _():
        acc_ref[...] = r_ref[...]

    acc_ref[...] += jnp.dot(a_ref[...], b_ref[...], preferred_element_type=F32)

    @pl.when(kk == pl.num_programs(2) - 1)
    def _():
        o_ref[...] = acc_ref[...].astype(o_ref.dtype)


def matmul_k(a, b, res, out_dtype, tm=1024, tn=512, tk=5504, name="matmul_k"):
    m, k = a.shape
    n = b.shape[1]
    tm = min(tm, m)
    tn = min(tn, n)
    tk = min(tk, k)
    return pl.pallas_call(
        _matmul_k_kernel,
        grid=(m // tm, n // tn, k // tk),
        in_specs=[pl.BlockSpec((tm, tk), lambda i, j, kk: (i, kk)),
                  pl.BlockSpec((tk, tn), lambda i, j, kk: (kk, j)),
                  pl.BlockSpec((tm, tn), lambda i, j, kk: (i, j))],
        out_specs=pl.BlockSpec((tm, tn), lambda i, j, kk: (i, j)),
        out_shape=jax.ShapeDtypeStruct((m, n), out_dtype),
        scratch_shapes=[pltpu.VMEM((tm, tn), F32)],
        compiler_params=_cparams(("parallel", "parallel", "arbitrary")),
        name=name,
    )(a, b, res)


def _select_kernel(q_ref, ke_ref, ko_ref, w_ref, side_ref, bias_ref, sideb_ref, s_ref, *,
                   tq, tk, n_chunks, topk):
    sideb_ref[...] = side_ref[...].astype(BF16)
    i = pl.program_id(1)
    t0 = i * tq
    nj = (t0 + tq + tk - 1) // tk
    qpos = t0 + lax.broadcasted_iota(jnp.int32, (1, tq), 1)
    w = w_ref[0]

    def score_chunk(j, carry):
        rmin, rmax = carry
        r0 = pl.multiple_of(j * tk, tk)
        ke = ke_ref[0, pl.ds(r0, tk), :]
        ko = ko_ref[0, pl.ds(r0, tk), :]
        acc = jnp.zeros((tk, tq), F32)
        for p in range(IDX_HEADS // 2):
            qp = q_ref[0, p]
            le = jnp.dot(ke, qp, preferred_element_type=F32)
            lo = jnp.dot(ko, qp, preferred_element_type=F32)
            acc = (acc + w[2 * p:2 * p + 1, :] * jnp.maximum(le, 0.0)
                   + w[2 * p + 1:2 * p + 2, :] * jnp.maximum(lo, 0.0))
        kpos = j * tk + lax.broadcasted_iota(jnp.int32, (tk, 1), 0)
        causal = kpos <= qpos
        s_ref[j] = jnp.where(causal, acc, NEG)
        rmax = jnp.maximum(rmax, jnp.max(jnp.where(causal, acc, NEG), axis=0, keepdims=True))
        rmin = jnp.minimum(rmin, jnp.min(jnp.where(causal, acc, BIG), axis=0, keepdims=True))
        return rmin, rmax

    rmin, rmax = lax.fori_loop(
        0, nj, score_chunk,
        (jnp.full((1, tq), BIG, F32), jnp.full((1, tq), NEG, F32)))

    n_causal = (qpos + 1).astype(F32)
    keff = jnp.minimum(n_causal, float(topk))
    nacc = 64

    def count_ge(thr):
        def body(j, c):
            part = jnp.where(s_ref[j] >= thr, 1.0, 0.0)
            return c + jnp.sum(part.reshape(tk // nacc, nacc, tq), axis=0)
        c = lax.fori_loop(0, nj, body, jnp.zeros((nacc, tq), F32))
        return jnp.sum(c, axis=0, keepdims=True)

    hi0 = rmax
    c_max = count_ge(hi0)
    at_max = c_max >= keff
    lo0 = jnp.where(at_max, hi0, rmin)
    c_lo0 = jnp.where(at_max, c_max, n_causal)

    def cond(st):
        it, _, _, c_lo = st
        return jnp.logical_and(it < BISECT_MAX_ITERS, jnp.max(jnp.abs(c_lo - keff)) > 0.0)

    def halve(lo, hi, c_lo):
        mid = 0.5 * (lo + hi)
        c = count_ge(mid)
        ge = c >= keff
        return jnp.where(ge, mid, lo), jnp.where(ge, hi, mid), jnp.where(ge, c, c_lo)

    def bisect(st):
        it, lo, hi, c_lo = st
        lo, hi, c_lo = halve(*halve(lo, hi, c_lo))
        return it + 2, lo, hi, c_lo

    _, lo, _, _ = lax.while_loop(cond, bisect, (jnp.int32(0), lo0, hi0, c_lo0))

    def write_sel(j, c):
        sel_t = jnp.where(s_ref[j] >= lo, 0.0, NEG)
        bias_ref[0, j] = sel_t.T.astype(bias_ref.dtype)
        return c

    def write_none(j, c):
        bias_ref[0, j] = jnp.full((tq, tk), NEG, bias_ref.dtype)
        return c

    lax.fori_loop(0, nj, write_sel, 0)
    lax.fori_loop(nj, n_chunks, write_none, 0)


def select_mask(qt, ke, ko, wt, side, *, tq, tk, topk):
    bsz, npair, _, seq = qt.shape
    n_chunks = seq // tk
    nqb = seq // tq
    side_spec, side_shape = _side_specs(side, bsz * nqb, lambda b, i: b * nqb + i)
    return pl.pallas_call(
        functools.partial(_select_kernel, tq=tq, tk=tk, n_chunks=n_chunks, topk=topk),
        grid=(bsz, nqb),
        in_specs=[pl.BlockSpec((1, npair, LANES, tq), lambda b, i: (b, 0, 0, i)),
                  pl.BlockSpec((1, seq, LANES), lambda b, i: (b, 0, 0)),
                  pl.BlockSpec((1, seq, LANES), lambda b, i: (b, 0, 0)),
                  pl.BlockSpec((1, IDX_HEADS, tq), lambda b, i: (b, 0, i)),
                  side_spec],
        out_specs=[pl.BlockSpec((1, n_chunks, tq, tk), lambda b, i: (b, 0, i, 0)), side_spec],
        out_shape=[jax.ShapeDtypeStruct((bsz, n_chunks, seq, tk), BF16), side_shape],
        scratch_shapes=[pltpu.VMEM((n_chunks, tk, tq), F32)],
        compiler_params=_cparams(("parallel", "parallel")),
        name="select_mask",
    )(qt, ke, ko, wt, side)


def _attn_kernel(q_ref, k_ref, vx_ref, b_ref, side_ref, o_ref, sideb_ref, m_ref, acc_ref, *,
                 tq, tk, nh, dh):
    sideb_ref[...] = side_ref[...].astype(BF16)
    i = pl.program_id(1)
    j = pl.program_id(2)

    @pl.when(j == 0)
    def _():
        m_ref[...] = jnp.full(m_ref.shape, -jnp.inf, F32)
        acc_ref[...] = jnp.zeros(acc_ref.shape, F32)

    @pl.when(j * tk <= i * tq + tq - 1)
    def _():
        bias = b_ref[0, 0]
        for h in range(nh):
            hs = slice(h * dh, (h + 1) * dh)
            xs = slice(2 * h * dh, (2 * h + 2) * dh)
            s = lax.dot_general(q_ref[:, hs], k_ref[:, hs], (((1,), (1,)), ((), ())),
                                preferred_element_type=F32)
            sb = s.astype(BF16) + bias
            part = sb[:, :LANES]
            for c in range(1, tk // LANES):
                part = jnp.maximum(part, sb[:, c * LANES:(c + 1) * LANES])
            m_old = m_ref[h]
            m_new = jnp.maximum(m_old, jnp.max(part.astype(F32), axis=1, keepdims=True))
            p = jnp.exp2(sb - m_new[:, :1].astype(BF16))
            alpha = jnp.exp2(m_old - m_new)
            pv = jnp.dot(p, vx_ref[:, xs], preferred_element_type=F32)
            acc_ref[:, xs] = jnp.concatenate([alpha, alpha], axis=1) * acc_ref[:, xs] + pv
            m_ref[h] = m_new

    @pl.when(j == pl.num_programs(2) - 1)
    def _():
        for h in range(nh):
            num = acc_ref[:, 2 * h * dh:(2 * h + 1) * dh]
            den = acc_ref[:, (2 * h + 1) * dh:(2 * h + 2) * dh]
            o_ref[:, h * dh:(h + 1) * dh] = (num / den).astype(o_ref.dtype)


def masked_attention(qk, vx, bias, side, *, bsz, seq, tq, tk):
    nh, dh = ATT_HEADS, HEAD_DIM
    d_att = nh * dh
    nqb, nkb = seq // tq, seq // tk

    def jc(i, j):
        return jnp.minimum(j, (i * tq + tq - 1) // tk)

    side_spec, side_shape = _side_specs(side, bsz * nqb * nkb,
                                        lambda b, i, j: (b * nqb + i) * nkb + j)
    return pl.pallas_call(
        functools.partial(_attn_kernel, tq=tq, tk=tk, nh=nh, dh=dh),
        grid=(bsz, nqb, nkb),
        in_specs=[pl.BlockSpec((tq, d_att), lambda b, i, j: (b * nqb + i, 0)),
                  pl.BlockSpec((tk, d_att), lambda b, i, j: (b * nkb + jc(i, j), 1)),
                  pl.BlockSpec((tk, 2 * d_att), lambda b, i, j: (b * nkb + jc(i, j), 0)),
                  pl.BlockSpec((1, 1, tq, tk), lambda b, i, j: (b, jc(i, j), i, 0)),
                  side_spec],
        out_specs=[pl.BlockSpec((tq, d_att), lambda b, i, j: (b * nqb + i, 0)), side_spec],
        out_shape=[jax.ShapeDtypeStruct((bsz * seq, d_att), BF16), side_shape],
        scratch_shapes=[pltpu.VMEM((nh, tq, LANES), F32),
                        pltpu.VMEM((tq, 2 * d_att), F32)],
        compiler_params=_cparams(("parallel", "parallel", "arbitrary")),
        name="masked_attention",
    )(qk, qk, vx, bias, side)


def _gelu_tanh(x):
    return 0.5 * x * (1.0 + jnp.tanh(0.7978845608028654 * (x + 0.044715 * (x * x * x))))


def _gmlp_kernel(u_ref, v_ref, gain_ref, w_ref, bias_ref, o_ref, *, n_sub):
    tri = (lax.broadcasted_iota(jnp.int32, (CHUNK, CHUNK), 1)
           <= lax.broadcasted_iota(jnp.int32, (CHUNK, CHUNK), 0))
    for g in range(GMLP_GROUPS):
        gs = slice(g * GMLP_CH, (g + 1) * GMLP_CH)
        wg = jnp.where(tri, w_ref[g], 0.0).astype(BF16)
        gain = gain_ref[:, gs]
        bias = bias_ref[:, gs]
        for c in range(n_sub):
            rs = slice(c * CHUNK, (c + 1) * CHUNK)
            u = _gelu_tanh(u_ref[rs, gs].astype(F32))
            v = _gelu_tanh(v_ref[rs, gs].astype(F32))
            mu = jnp.mean(v, axis=-1, keepdims=True)
            vc = v - mu
            var = jnp.mean(vc * vc, axis=-1, keepdims=True)
            vn = (vc * lax.rsqrt(var + EPS) * gain).astype(BF16)
            z = jnp.dot(wg, vn, preferred_element_type=F32) + bias
            o_ref[rs, gs] = (u * z).astype(o_ref.dtype)


def gmlp_gating(uv, v_gain, w_s, b_s, tm=512):
    m = uv.shape[0]
    d = GMLP_GROUPS * GMLP_CH
    tm = min(tm, m)
    gain = v_gain.reshape(1, d).astype(F32)
    bias_full = jnp.repeat(b_s.T.astype(F32), GMLP_CH, axis=1)
    return pl.pallas_call(
        functools.partial(_gmlp_kernel, n_sub=tm // CHUNK),
        grid=(m // tm,),
        in_specs=[pl.BlockSpec((tm, d), lambda i: (i, 0)),
                  pl.BlockSpec((tm, d), lambda i: (i, 1)),
                  pl.BlockSpec((1, d), lambda i: (0, 0)),
                  pl.BlockSpec((GMLP_GROUPS, CHUNK, CHUNK), lambda i: (0, 0, 0)),
                  pl.BlockSpec((CHUNK, d), lambda i: (0, 0))],
        out_specs=pl.BlockSpec((tm, d), lambda i: (i, 0)),
        out_shape=jax.ShapeDtypeStruct((m, d), BF16),
        compiler_params=_cparams(("parallel",)),
        name="gmlp_gating",
    )(uv, uv, gain, w_s.astype(F32), bias_full)


def dsa_attention(qk, vx, qt, proj_small, side_a, side_b, *, bsz, seq,
                  tq_sel=256, tq_att=512, tk=512):
    tq_sel = min(tq_sel, seq)
    tq_att = min(tq_att, seq)
    tk = min(tk, seq)
    topk = min(TOPK_MAX, seq // 4)
    kx = proj_small[:, :IDX_DIM].astype(BF16).reshape(bsz, seq, IDX_DIM)
    zeros = jnp.zeros_like(kx)
    ke = jnp.concatenate([kx, zeros], axis=2)
    ko = jnp.concatenate([zeros, kx], axis=2)
    wt = proj_small[:, IDX_DIM:IDX_DIM + IDX_HEADS].reshape(bsz, seq, IDX_HEADS).transpose(0, 2, 1)
    bias, side_a16 = select_mask(qt, ke, ko, wt, side_a, tq=tq_sel, tk=tk, topk=topk)
    att, side_b16 = masked_attention(qk, vx, bias, side_b, bsz=bsz, seq=seq, tq=tq_att, tk=tk)
    return att, side_a16, side_b16


def _in_proj_scale(d_att, n_idx, d_gm):
    return jnp.concatenate([
        jnp.full((d_att,), LOG2E * HEAD_DIM ** -0.5, F32),
        jnp.ones((2 * d_att,), F32),
        jnp.full((n_idx,), IDX_DIM ** -0.5, F32),
        jnp.ones((IDX_DIM,), F32),
        jnp.full((IDX_HEADS,), IDX_HEADS ** -0.5, F32),
        jnp.ones((2 * d_gm,), F32)]).reshape(-1, 1)


def kernel(x, norm_mix, w_in, gmlp_v_gain, w_spatial, b_spatial, w_out, norm_ffn,
           w_gate, w_up, w_down, norm_final):
    bsz, seq, d_model = x.shape
    m = bsz * seq
    d_att = ATT_HEADS * HEAD_DIM
    d_gm = GMLP_GROUPS * GMLP_CH
    n_idx = IDX_HEADS * IDX_DIM
    depth = w_in.shape[0]
    tn = 512
    scale = _in_proj_scale(d_att, n_idx, d_gm)
    xf = x.reshape(m, d_model)
    for i in range(depth):
        wt = jnp.swapaxes(w_in[i], 0, 1)
        uv_off = 3 * d_att + n_idx + IDX_DIM + IDX_HEADS
        w_uv = wt[uv_off:].astype(BF16)

        h = rmsnorm(xf, norm_mix[i], BF16)
        qk = matmul([h], [(wt, 0)], BF16, n=2 * d_att, col_off=0, w_t=True, scale=scale,
                    tn=tn, name="proj_qk")
        vx = matmul([h], [(wt, 0)], BF16, n=d_att, col_off=2 * d_att // tn, w_t=True, scale=scale,
                    out_mode="ones", tn=tn, name="proj_v")
        qt = matmul([h], [(wt, 0)], BF16, n=n_idx, col_off=3 * d_att // tn, w_t=True, scale=scale,
                    out_mode="slabs_t", seq=seq, tn=tn, name="proj_qidx")
        proj_small = matmul([h], [(wt, 0)], F32, n=LANES, col_off=(3 * d_att + n_idx) // LANES,
                            w_t=True, scale=scale, tn=LANES, name="proj_small")
        proj_uv, wo = matmul([h], [(w_uv, 0)], BF16, n=2 * d_gm, w_t=True, side=w_out[i],
                             tn=1024, name="proj_uv")

        att, wg, wu = dsa_attention(qk, vx, qt, proj_small, w_gate[i], w_up[i], bsz=bsz, seq=seq)
        gm = gmlp_gating(proj_uv, gmlp_v_gain[i], w_spatial[i], b_spatial[i])

        xf = matmul([att, gm], [(wo, 0), (wo, 1)], F32, n=d_model, res=xf, tn=1024, name="out_proj")

        h2 = rmsnorm(xf, norm_ffn[i], BF16)
        hid, wd = swiglu_up(h2, wg, wu, w_down[i])
        xf = matmul_k(hid, wd, xf, F32, name="ffn_down")
    out = rmsnorm(xf, norm_final, x.dtype)
    return out.reshape(bsz, seq, d_model)
```

```python
import functools

import jax
import jax.numpy as jnp
from jax import lax
from jax.experimental import pallas as pl
from jax.experimental.pallas import tpu as pltpu

F32 = jnp.float32
BF16 = jnp.bfloat16

ATT_HEADS = 16
HEAD_DIM = 128
IDX_HEADS = 32
IDX_DIM = 64
TOPK_MAX = 256
GMLP_GROUPS = 16
GMLP_CH = 128
CHUNK = 128
EPS = 1e-6
NEG = -1e30
BIG = 1e30
LOG2E = 1.4426950408889634

LANES = 128
F32_SUBLANES = 8
BF16_SUBLANES = 16
ONES_ROWS = BF16_SUBLANES
VMEM_LIMIT = 56 * 1024 * 1024
BISECT_MAX_ITERS = 48


def _cparams(sem):
    return pltpu.CompilerParams(dimension_semantics=sem, vmem_limit_bytes=VMEM_LIMIT)


def _rmsnorm_kernel(x_ref, g_ref, o_ref):
    x = x_ref[...].astype(F32)
    ms = jnp.mean(x * x, axis=-1, keepdims=True)
    o_ref[...] = (x * lax.rsqrt(ms + EPS) * g_ref[...]).astype(o_ref.dtype)


def rmsnorm(x, g, out_dtype, tm=256):
    m, d = x.shape
    tm = min(tm, m)
    return pl.pallas_call(
        _rmsnorm_kernel,
        grid=(m // tm,),
        in_specs=[pl.BlockSpec((tm, d), lambda i: (i, 0)),
                  pl.BlockSpec((1, d), lambda i: (0, 0))],
        out_specs=pl.BlockSpec((tm, d), lambda i: (i, 0)),
        out_shape=jax.ShapeDtypeStruct((m, d), out_dtype),
        compiler_params=_cparams(("parallel",)),
        name="rmsnorm",
    )(x, g.reshape(1, d).astype(F32))


def _matmul_kernel(*refs, n_pairs, has_scale, has_res, has_side, cast_w, w_t, out_mode):
    a_refs = refs[:n_pairs]
    w_refs = refs[n_pairs:2 * n_pairs]
    pos = 2 * n_pairs
    sc_ref = refs[pos] if has_scale else None
    pos += int(has_scale)
    r_ref = refs[pos] if has_res else None
    pos += int(has_res)
    side_ref = refs[pos] if has_side else None
    pos += int(has_side)
    o_ref = refs[pos]
    pos += 1
    if has_side:
        refs[pos][...] = side_ref[...].astype(BF16)
        pos += 1
    wb_refs = refs[pos:]

    if cast_w:
        @pl.when(pl.program_id(1) == 0)
        def _():
            for p in range(n_pairs):
                w = w_refs[p][...]
                if has_scale:
                    w = w * sc_ref[...]
                wb_refs[p][...] = w.astype(BF16)
        rhs = wb_refs
    else:
        rhs = w_refs

    nt = (((1,), (1,)), ((), ()))
    if out_mode in ("slabs_t", "slabs_t_ones"):
        acc = lax.dot_general(rhs[0][...], a_refs[0][...], nt, preferred_element_type=F32)
        for pp in range(acc.shape[0] // LANES):
            o_ref[0, pp, :LANES, :] = acc[pp * LANES:(pp + 1) * LANES, :].astype(o_ref.dtype)
            if out_mode == "slabs_t_ones":
                o_ref[0, pp, LANES:, :] = jnp.ones((ONES_ROWS, acc.shape[1]), o_ref.dtype)
        return

    def mm(p):
        if w_t:
            return lax.dot_general(a_refs[p][...], rhs[p][...], nt, preferred_element_type=F32)
        return jnp.dot(a_refs[p][...], rhs[p][...], preferred_element_type=F32)

    acc = mm(0)
    for p in range(1, n_pairs):
        acc = acc + mm(p)
    if has_res:
        acc = acc + r_ref[...]

    o_ref[...] = acc.astype(o_ref.dtype)


def matmul(a_list, w_list, out_dtype, *, n, col_off=0, w_t=False, scale=None, res=None,
           side=None, out_mode="plain", seq=None, tm=1024, tn=512, name="matmul"):
    n_pairs = len(a_list)
    m = a_list[0].shape[0]
    tm = min(tm, m)
    tn = min(tn, n)
    cast_w = w_list[0][0].dtype != BF16
    in_specs, args, scratch = [], [], []
    for a in a_list:
        in_specs.append(pl.BlockSpec((tm, a.shape[1]), lambda j, i: (i, 0)))
        args.append(a)
    for a, (w, rb) in zip(a_list, w_list):
        k = a.shape[1]
        if w_t:
            in_specs.append(pl.BlockSpec((tn, k), lambda j, i: (j + col_off, 0)))
        else:
            in_specs.append(pl.BlockSpec((k, tn), lambda j, i, rb=rb: (rb, j + col_off)))
        args.append(w)
        if cast_w:
            scratch.append(pltpu.VMEM((tn, k) if w_t else (k, tn), BF16))
    if scale is not None:
        if w_t:
            in_specs.append(pl.BlockSpec((tn, 1), lambda j, i: (j + col_off, 0)))
        else:
            in_specs.append(pl.BlockSpec((1, tn), lambda j, i: (0, j + col_off)))
        args.append(scale)
    if res is not None:
        in_specs.append(pl.BlockSpec((tm, tn), lambda j, i: (i, j)))
        args.append(res)
    nmb_all = m // tm
    if side is not None:
        side_spec, side_shape = _side_specs(side, (n // tn) * nmb_all, lambda j, i: j * nmb_all + i)
        in_specs.append(side_spec)
        args.append(side)
    if out_mode == "plain":
        out_spec = pl.BlockSpec((tm, tn), lambda j, i: (i, j))
        out_shape = jax.ShapeDtypeStruct((m, n), out_dtype)
    else:
        assert w_t and n_pairs == 1 and res is None
        nmb = seq // tm
        rows = LANES + (ONES_ROWS if out_mode == "slabs_t_ones" else 0)
        out_spec = pl.BlockSpec((1, tn // LANES, rows, tm), lambda j, i: (i // nmb, j, 0, i % nmb))
        out_shape = jax.ShapeDtypeStruct((m // seq, n // LANES, rows, seq), out_dtype)
    if side is not None:
        out_spec, out_shape = [out_spec, side_spec], [out_shape, side_shape]
    return pl.pallas_call(
        functools.partial(_matmul_kernel, n_pairs=n_pairs, has_scale=scale is not None,
                          has_res=res is not None, has_side=side is not None,
                          cast_w=cast_w, w_t=w_t, out_mode=out_mode),
        grid=(n // tn, m // tm),
        in_specs=in_specs,
        out_specs=out_spec,
        out_shape=out_shape,
        scratch_shapes=scratch,
        compiler_params=_cparams(("parallel", "arbitrary")),
        name=name,
    )(*args)


def _side_specs(w, steps, step_of):
    rows, cols = w.shape
    slab = rows // steps
    assert slab * steps == rows and slab % BF16_SUBLANES == 0, (rows, steps)
    spec = pl.BlockSpec((slab, cols), lambda *g: (step_of(*g), 0))
    return spec, jax.ShapeDtypeStruct((rows, cols), BF16)


def _swiglu_up_kernel(h_ref, wg_ref, wu_ref, wd_ref, o_ref, wdb_ref):
    wdb_ref[...] = wd_ref[...].astype(BF16)
    h = h_ref[...]
    g = jnp.dot(h, wg_ref[...], preferred_element_type=F32)
    u = jnp.dot(h, wu_ref[...], preferred_element_type=F32)
    o_ref[...] = (g * jax.nn.sigmoid(g) * u).astype(o_ref.dtype)


def swiglu_up(h, wg, wu, wd, tm=2048, tn=256):
    m, k = h.shape
    n = wg.shape[1]
    tm = min(tm, m)
    tn = min(tn, n)
    nnb = n // tn
    side_spec, side_shape = _side_specs(wd, (m // tm) * nnb, lambda i, j: i * nnb + j)
    return pl.pallas_call(
        _swiglu_up_kernel,
        grid=(m // tm, nnb),
        in_specs=[pl.BlockSpec((tm, k), lambda i, j: (i, 0)),
                  pl.BlockSpec((k, tn), lambda i, j: (0, j)),
                  pl.BlockSpec((k, tn), lambda i, j: (0, j)),
                  side_spec],
        out_specs=[pl.BlockSpec((tm, tn), lambda i, j: (i, j)), side_spec],
        out_shape=[jax.ShapeDtypeStruct((m, n), BF16), side_shape],
        compiler_params=_cparams(("parallel", "parallel")),
        name="swiglu_up",
    )(h, wg, wu, wd)


def _matmul_k_kernel(a_ref, b_ref, r_ref, o_ref, acc_ref):
    kk = pl.program_id(2)

    @pl.when(kk == 0)
    def _():
        acc_ref[...] = r_ref[...]

    acc_ref[...] += jnp.dot(a_ref[...], b_ref[...], preferred_element_type=F32)

    @pl.when(kk == pl.num_programs(2) - 1)
    def _():
        o_ref[...] = acc_ref[...].astype(o_ref.dtype)


def matmul_k(a, b, res, out_dtype, tm=1024, tn=512, tk=5504, name="matmul_k"):
    m, k = a.shape
    n = b.shape[1]
    tm = min(tm, m)
    tn = min(tn, n)
    tk = min(tk, k)
    return pl.pallas_call(
        _matmul_k_kernel,
        grid=(m // tm, n // tn, k // tk),
        in_specs=[pl.BlockSpec((tm, tk), lambda i, j, kk: (i, kk)),
                  pl.BlockSpec((tk, tn), lambda i, j, kk: (kk, j)),
                  pl.BlockSpec((tm, tn), lambda i, j, kk: (i, j))],
        out_specs=pl.BlockSpec((tm, tn), lambda i, j, kk: (i, j)),
        out_shape=jax.ShapeDtypeStruct((m, n), out_dtype),
        scratch_shapes=[pltpu.VMEM((tm, tn), F32)],
        compiler_params=_cparams(("parallel", "parallel", "arbitrary")),
        name=name,
    )(a, b, res)


def _select_kernel(q_ref, ke_ref, ko_ref, w_ref, sa_ref, sb_ref, bias_ref, sa16_ref, sb16_ref,
                   s_ref, *, tq, tk, n_chunks, topk):
    sa16_ref[...] = sa_ref[...].astype(BF16)
    sb16_ref[...] = sb_ref[...].astype(BF16)
    i = pl.program_id(1)
    t0 = i * tq
    nj = (t0 + tq + tk - 1) // tk
    qpos = t0 + lax.broadcasted_iota(jnp.int32, (1, tq), 1)
    w = w_ref[0]

    def score_chunk(j, carry):
        rmin, rmax = carry
        r0 = pl.multiple_of(j * tk, tk)
        ke = ke_ref[0, pl.ds(r0, tk), :]
        ko = ko_ref[0, pl.ds(r0, tk), :]
        acc = jnp.zeros((tk, tq), F32)
        for p in range(IDX_HEADS // 2):
            qp = q_ref[0, p]
            le = jnp.dot(ke, qp, preferred_element_type=F32)
            lo = jnp.dot(ko, qp, preferred_element_type=F32)
            acc = (acc + w[2 * p:2 * p + 1, :] * jnp.maximum(le, 0.0)
                   + w[2 * p + 1:2 * p + 2, :] * jnp.maximum(lo, 0.0))
        kpos = j * tk + lax.broadcasted_iota(jnp.int32, (tk, 1), 0)
        causal = kpos <= qpos
        s_ref[j] = jnp.where(causal, acc, NEG)
        rmax = jnp.maximum(rmax, jnp.max(jnp.where(causal, acc, NEG), axis=0, keepdims=True))
        rmin = jnp.minimum(rmin, jnp.min(jnp.where(causal, acc, BIG), axis=0, keepdims=True))
        return rmin, rmax

    rmin, rmax = lax.fori_loop(
        0, nj, score_chunk,
        (jnp.full((1, tq), BIG, F32), jnp.full((1, tq), NEG, F32)))

    n_causal = (qpos + 1).astype(F32)
    keff = jnp.minimum(n_causal, float(topk))
    nacc = 64

    def count_ge(thr):
        def body(j, c):
            part = jnp.where(s_ref[j] >= thr, 1.0, 0.0)
            return c + jnp.sum(part.reshape(tk // nacc, nacc, tq), axis=0)
        c = lax.fori_loop(0, nj, body, jnp.zeros((nacc, tq), F32))
        return jnp.sum(c, axis=0, keepdims=True)

    hi0 = rmax
    c_max = count_ge(hi0)
    at_max = c_max >= keff
    lo0 = jnp.where(at_max, hi0, rmin)
    c_lo0 = jnp.where(at_max, c_max, n_causal)

    def cond(st):
        it, _, _, c_lo = st
        return jnp.logical_and(it < BISECT_MAX_ITERS, jnp.max(jnp.abs(c_lo - keff)) > 0.0)

    def halve(lo, hi, c_lo):
        mid = 0.5 * (lo + hi)
        c = count_ge(mid)
        ge = c >= keff
        return jnp.where(ge, mid, lo), jnp.where(ge, hi, mid), jnp.where(ge, c, c_lo)

    def bisect(st):
        it, lo, hi, c_lo = st
        lo, hi, c_lo = halve(*halve(lo, hi, c_lo))
        return it + 2, lo, hi, c_lo

    _, lo, _, _ = lax.while_loop(cond, bisect, (jnp.int32(0), lo0, hi0, c_lo0))

    def write_sel(j, c):
        r0 = pl.multiple_of(j * tk, tk)
        bias_ref[0, pl.ds(r0, tk), :] = jnp.where(s_ref[j] >= lo, 0.0, NEG).astype(bias_ref.dtype)
        return c

    def write_none(j, c):
        r0 = pl.multiple_of(j * tk, tk)
        bias_ref[0, pl.ds(r0, tk), :] = jnp.full((tk, tq), NEG, bias_ref.dtype)
        return c

    lax.fori_loop(0, nj, write_sel, 0)
    lax.fori_loop(nj, n_chunks, write_none, 0)


def select_mask(qt, ke, ko, wt, side_a, side_b, *, tq, tk, topk):
    bsz, npair, _, seq = qt.shape
    n_chunks = seq // tk
    nqb = seq // tq
    step_of = lambda b, i: b * nqb + i
    spec_a, shape_a = _side_specs(side_a, bsz * nqb, step_of)
    spec_b, shape_b = _side_specs(side_b, bsz * nqb, step_of)
    return pl.pallas_call(
        functools.partial(_select_kernel, tq=tq, tk=tk, n_chunks=n_chunks, topk=topk),
        grid=(bsz, nqb),
        in_specs=[pl.BlockSpec((1, npair, LANES, tq), lambda b, i: (b, 0, 0, i)),
                  pl.BlockSpec((1, seq, LANES), lambda b, i: (b, 0, 0)),
                  pl.BlockSpec((1, seq, LANES), lambda b, i: (b, 0, 0)),
                  pl.BlockSpec((1, IDX_HEADS, tq), lambda b, i: (b, 0, i)),
                  spec_a, spec_b],
        out_specs=[pl.BlockSpec((1, seq, tq), lambda b, i: (b, 0, i)), spec_a, spec_b],
        out_shape=[jax.ShapeDtypeStruct((bsz, seq, seq), BF16), shape_a, shape_b],
        scratch_shapes=[pltpu.VMEM((n_chunks, tk, tq), F32)],
        compiler_params=_cparams(("parallel", "parallel")),
        name="select_mask",
    )(qt, ke, ko, wt, side_a, side_b)


def _attn_kernel(q_ref, k_ref, v_ref, b_ref, o_ref, m_ref, acc_ref, *, tq, tk, nh, dh):
    i = pl.program_id(1)
    j = pl.program_id(2)

    @pl.when(j == 0)
    def _():
        m_ref[...] = jnp.full(m_ref.shape, -jnp.inf, F32)
        acc_ref[...] = jnp.zeros(acc_ref.shape, F32)

    @pl.when(j * tk <= i * tq + tq - 1)
    def _():
        bias = b_ref[0]
        for h in range(nh):
            s = jnp.dot(k_ref[:, h * dh:(h + 1) * dh], q_ref[0, h], preferred_element_type=F32)
            sb = s.astype(BF16) + bias
            part = jnp.max(sb.reshape(tk // BF16_SUBLANES, BF16_SUBLANES, tq), axis=0)
            m_blk = jnp.max(part.astype(F32), axis=0, keepdims=True)
            m_old = m_ref[h]
            m_new = jnp.maximum(m_old, m_blk)
            p = jnp.exp2(sb - m_new[:1].astype(BF16))
            alpha = jnp.exp2(m_old - m_new)
            pv = jnp.dot(v_ref[0, h], p, preferred_element_type=F32)
            acc_ref[h] = alpha[:1] * acc_ref[h] + pv
            m_ref[h] = m_new

    @pl.when(j == pl.num_programs(2) - 1)
    def _():
        for h in range(nh):
            acc = acc_ref[h]
            o = acc[:dh] / acc[dh:dh + 1]
            o_ref[:, h * dh:(h + 1) * dh] = o.T.astype(o_ref.dtype)


def masked_attention(qt, kx, vt, bias, *, bsz, seq, tq, tk):
    nh, dh = ATT_HEADS, HEAD_DIM
    d_att = nh * dh
    nqb, nkb = seq // tq, seq // tk
    vrows = dh + ONES_ROWS

    def jc(i, j):
        return jnp.minimum(j, (i * tq + tq - 1) // tk)

    return pl.pallas_call(
        functools.partial(_attn_kernel, tq=tq, tk=tk, nh=nh, dh=dh),
        grid=(bsz, nqb, nkb),
        in_specs=[pl.BlockSpec((1, nh, dh, tq), lambda b, i, j: (b, 0, 0, i)),
                  pl.BlockSpec((tk, d_att), lambda b, i, j: (b * nkb + jc(i, j), 0)),
                  pl.BlockSpec((1, nh, vrows, tk), lambda b, i, j: (b, 0, 0, jc(i, j))),
                  pl.BlockSpec((1, tk, tq), lambda b, i, j: (b, jc(i, j), i))],
        out_specs=pl.BlockSpec((tq, d_att), lambda b, i, j: (b * nqb + i, 0)),
        out_shape=jax.ShapeDtypeStruct((bsz * seq, d_att), BF16),
        scratch_shapes=[pltpu.VMEM((nh, F32_SUBLANES, tq), F32),
                        pltpu.VMEM((nh, vrows, tq), F32)],
        compiler_params=_cparams(("parallel", "parallel", "arbitrary")),
        name="masked_attention",
    )(qt, kx, vt, bias)


def _gelu_tanh(x):
    return 0.5 * x * (1.0 + jnp.tanh(0.7978845608028654 * (x + 0.044715 * (x * x * x))))


def _gmlp_kernel(u_ref, v_ref, gain_ref, w_ref, bias_ref, o_ref, *, n_sub):
    tri = (lax.broadcasted_iota(jnp.int32, (CHUNK, CHUNK), 1)
           <= lax.broadcasted_iota(jnp.int32, (CHUNK, CHUNK), 0))
    for g in range(GMLP_GROUPS):
        gs = slice(g * GMLP_CH, (g + 1) * GMLP_CH)
        wg = jnp.where(tri, w_ref[g], 0.0).astype(BF16)
        gain = gain_ref[:, gs]
        bias = bias_ref[:, gs]
        for c in range(n_sub):
            rs = slice(c * CHUNK, (c + 1) * CHUNK)
            u = _gelu_tanh(u_ref[rs, gs].astype(F32))
            v = _gelu_tanh(v_ref[rs, gs].astype(F32))
            mu = jnp.mean(v, axis=-1, keepdims=True)
            vc = v - mu
            var = jnp.mean(vc * vc, axis=-1, keepdims=True)
            vn = (vc * lax.rsqrt(var + EPS) * gain).astype(BF16)
            z = jnp.dot(wg, vn, preferred_element_type=F32) + bias
            o_ref[rs, gs] = (u * z).astype(o_ref.dtype)


def gmlp_gating(uv, v_gain, w_s, b_s, tm=512):
    m = uv.shape[0]
    d = GMLP_GROUPS * GMLP_CH
    tm = min(tm, m)
    gain = v_gain.reshape(1, d).astype(F32)
    bias_full = jnp.repeat(b_s.T.astype(F32), GMLP_CH, axis=1)
    return pl.pallas_call(
        functools.partial(_gmlp_kernel, n_sub=tm // CHUNK),
        grid=(m // tm,),
        in_specs=[pl.BlockSpec((tm, d), lambda i: (i, 0)),
                  pl.BlockSpec((tm, d), lambda i: (i, 1)),
                  pl.BlockSpec((1, d), lambda i: (0, 0)),
                  pl.BlockSpec((GMLP_GROUPS, CHUNK, CHUNK), lambda i: (0, 0, 0)),
                  pl.BlockSpec((CHUNK, d), lambda i: (0, 0))],
        out_specs=pl.BlockSpec((tm, d), lambda i: (i, 0)),
        out_shape=jax.ShapeDtypeStruct((m, d), BF16),
        compiler_params=_cparams(("parallel",)),
        name="gmlp_gating",
    )(uv, uv, gain, w_s.astype(F32), bias_full)


def dsa_attention(qt, kx, vt, qit, proj_small, side_a, side_b, *, bsz, seq,
                  tq_sel=256, tq_att=1024, tk_sel=512, tk_att=1024):
    tq_sel, tk_sel = min(tq_sel, seq), min(tk_sel, seq)
    tq_att, tk_att = min(tq_att, seq), min(tk_att, seq)
    topk = min(TOPK_MAX, seq // 4)
    ki = proj_small[:, :IDX_DIM].astype(BF16).reshape(bsz, seq, IDX_DIM)
    zeros = jnp.zeros_like(ki)
    ke = jnp.concatenate([ki, zeros], axis=2)
    ko = jnp.concatenate([zeros, ki], axis=2)
    wt = proj_small[:, IDX_DIM:IDX_DIM + IDX_HEADS].reshape(bsz, seq, IDX_HEADS).transpose(0, 2, 1)
    bias, side_a16, side_b16 = select_mask(qit, ke, ko, wt, side_a, side_b,
                                           tq=tq_sel, tk=tk_sel, topk=topk)
    att = masked_attention(qt, kx, vt, bias, bsz=bsz, seq=seq, tq=tq_att, tk=tk_att)
    return att, side_a16, side_b16


def _in_proj_scale(d_att, n_idx, d_gm):
    return jnp.concatenate([
        jnp.full((d_att,), LOG2E * HEAD_DIM ** -0.5, F32),
        jnp.ones((2 * d_att,), F32),
        jnp.full((n_idx,), IDX_DIM ** -0.5, F32),
        jnp.ones((IDX_DIM,), F32),
        jnp.full((IDX_HEADS,), IDX_HEADS ** -0.5, F32),
        jnp.ones((2 * d_gm,), F32)]).reshape(-1, 1)


def kernel(x, norm_mix, w_in, gmlp_v_gain, w_spatial, b_spatial, w_out, norm_ffn,
           w_gate, w_up, w_down, norm_final):
    bsz, seq, d_model = x.shape
    m = bsz * seq
    d_att = ATT_HEADS * HEAD_DIM
    d_gm = GMLP_GROUPS * GMLP_CH
    n_idx = IDX_HEADS * IDX_DIM
    depth = w_in.shape[0]
    tn = 512
    scale = _in_proj_scale(d_att, n_idx, d_gm)
    xf = x.reshape(m, d_model)
    for i in range(depth):
        wt = jnp.swapaxes(w_in[i], 0, 1)
        uv_off = 3 * d_att + n_idx + IDX_DIM + IDX_HEADS
        w_uv = wt[uv_off:].astype(BF16)

        h = rmsnorm(xf, norm_mix[i], BF16)
        qt = matmul([h], [(wt, 0)], BF16, n=d_att, col_off=0, w_t=True, scale=scale,
                    out_mode="slabs_t", seq=seq, tn=tn, name="proj_q")
        kx = matmul([h], [(wt, 0)], BF16, n=d_att, col_off=d_att // tn, w_t=True, scale=scale,
                    tn=tn, name="proj_k")
        vt = matmul([h], [(wt, 0)], BF16, n=d_att, col_off=2 * d_att // tn, w_t=True, scale=scale,
                    out_mode="slabs_t_ones", seq=seq, tn=tn, name="proj_v")
        qit = matmul([h], [(wt, 0)], BF16, n=n_idx, col_off=3 * d_att // tn, w_t=True, scale=scale,
                     out_mode="slabs_t", seq=seq, tn=tn, name="proj_qidx")
        proj_small = matmul([h], [(wt, 0)], F32, n=LANES, col_off=(3 * d_att + n_idx) // LANES,
                            w_t=True, scale=scale, tn=LANES, name="proj_small")
        proj_uv, wo = matmul([h], [(w_uv, 0)], BF16, n=2 * d_gm, w_t=True, side=w_out[i],
                             tn=1024, name="proj_uv")

        att, wg, wu = dsa_attention(qt, kx, vt, qit, proj_small, w_gate[i], w_up[i],
                                    bsz=bsz, seq=seq)
        gm = gmlp_gating(proj_uv, gmlp_v_gain[i], w_spatial[i], b_spatial[i])

        xf = matmul([att, gm], [(wo, 0), (wo, 1)], F32, n=d_model, res=xf, tn=1024, name="out_proj")

        h2 = rmsnorm(xf, norm_ffn[i], BF16)
        hid, wd = swiglu_up(h2, wg, wu, w_down[i])
        xf = matmul_k(hid, wd, xf, F32, name="ffn_down")
    out = rmsnorm(xf, norm_final, x.dtype)
    return out.reshape(bsz, seq, d_model)
```

```python
import functools

import jax
import jax.numpy as jnp
from jax import lax
from jax.experimental import pallas as pl
from jax.experimental.pallas import tpu as pltpu

F32 = jnp.float32
BF16 = jnp.bfloat16

ATT_HEADS = 16
HEAD_DIM = 128
IDX_HEADS = 32
IDX_DIM = 64
TOPK_MAX = 256
GMLP_GROUPS = 16
GMLP_CH = 128
CHUNK = 128
EPS = 1e-6
NEG = -1e30
BIG = 1e30
LOG2E = 1.4426950408889634

LANES = 128
F32_SUBLANES = 8
BF16_SUBLANES = 16
ONES_ROWS = BF16_SUBLANES
VMEM_LIMIT = 56 * 1024 * 1024
BISECT_MAX_ITERS = 48


def _cparams(sem):
    return pltpu.CompilerParams(dimension_semantics=sem, vmem_limit_bytes=VMEM_LIMIT)


def _rmsnorm_kernel(x_ref, g_ref, o_ref):
    x = x_ref[...].astype(F32)
    ms = jnp.mean(x * x, axis=-1, keepdims=True)
    o_ref[...] = (x * lax.rsqrt(ms + EPS) * g_ref[...]).astype(o_ref.dtype)


def rmsnorm(x, g, out_dtype, tm=512):
    m, d = x.shape
    tm = min(tm, m)
    return pl.pallas_call(
        _rmsnorm_kernel,
        grid=(m // tm,),
        in_specs=[pl.BlockSpec((tm, d), lambda i: (i, 0)),
                  pl.BlockSpec((1, d), lambda i: (0, 0))],
        out_specs=pl.BlockSpec((tm, d), lambda i: (i, 0)),
        out_shape=jax.ShapeDtypeStruct((m, d), out_dtype),
        compiler_params=_cparams(("parallel",)),
        name="rmsnorm",
    )(x, g.reshape(1, d).astype(F32))


def _matmul_kernel(*refs, n_pairs, has_scale, has_res, has_side, cast_w, w_t, out_mode):
    a_refs = refs[:n_pairs]
    w_refs = refs[n_pairs:2 * n_pairs]
    pos = 2 * n_pairs
    sc_ref = refs[pos] if has_scale else None
    pos += int(has_scale)
    r_ref = refs[pos] if has_res else None
    pos += int(has_res)
    side_ref = refs[pos] if has_side else None
    pos += int(has_side)
    o_ref = refs[pos]
    pos += 1
    if has_side:
        refs[pos][...] = side_ref[...].astype(BF16)
        pos += 1
    wb_refs = refs[pos:]

    if cast_w:
        @pl.when(pl.program_id(1) == 0)
        def _():
            for p in range(n_pairs):
                w = w_refs[p][...]
                if has_scale:
                    w = w * sc_ref[...]
                wb_refs[p][...] = w.astype(BF16)
        rhs = wb_refs
    else:
        rhs = w_refs

    nt = (((1,), (1,)), ((), ()))
    if out_mode in ("slabs_t", "slabs_t_ones"):
        acc = lax.dot_general(rhs[0][...], a_refs[0][...], nt, preferred_element_type=F32)
        for pp in range(acc.shape[0] // LANES):
            o_ref[0, pp, :LANES, :] = acc[pp * LANES:(pp + 1) * LANES, :].astype(o_ref.dtype)
            if out_mode == "slabs_t_ones":
                o_ref[0, pp, LANES:, :] = jnp.ones((ONES_ROWS, acc.shape[1]), o_ref.dtype)
        return

    def mm(p):
        if w_t:
            return lax.dot_general(a_refs[p][...], rhs[p][...], nt, preferred_element_type=F32)
        return jnp.dot(a_refs[p][...], rhs[p][...], preferred_element_type=F32)

    acc = mm(0)
    for p in range(1, n_pairs):
        acc = acc + mm(p)
    if has_res:
        acc = acc + r_ref[...]

    o_ref[...] = acc.astype(o_ref.dtype)


def matmul(a_list, w_list, out_dtype, *, n, col_off=0, w_t=False, scale=None, res=None,
           side=None, out_mode="plain", seq=None, tm=1024, tn=512, name="matmul"):
    n_pairs = len(a_list)
    m = a_list[0].shape[0]
    tm = min(tm, m)
    tn = min(tn, n)
    cast_w = w_list[0][0].dtype != BF16
    in_specs, args, scratch = [], [], []
    for a in a_list:
        in_specs.append(pl.BlockSpec((tm, a.shape[1]), lambda j, i: (i, 0)))
        args.append(a)
    for a, (w, rb) in zip(a_list, w_list):
        k = a.shape[1]
        if w_t:
            in_specs.append(pl.BlockSpec((tn, k), lambda j, i: (j + col_off, 0)))
        else:
            in_specs.append(pl.BlockSpec((k, tn), lambda j, i, rb=rb: (rb, j + col_off)))
        args.append(w)
        if cast_w:
            scratch.append(pltpu.VMEM((tn, k) if w_t else (k, tn), BF16))
    if scale is not None:
        if w_t:
            in_specs.append(pl.BlockSpec((tn, 1), lambda j, i: (j + col_off, 0)))
        else:
            in_specs.append(pl.BlockSpec((1, tn), lambda j, i: (0, j + col_off)))
        args.append(scale)
    if res is not None:
        in_specs.append(pl.BlockSpec((tm, tn), lambda j, i: (i, j)))
        args.append(res)
    nmb_all = m // tm
    if side is not None:
        side_spec, side_shape = _side_specs(side, (n // tn) * nmb_all, lambda j, i: j * nmb_all + i)
        in_specs.append(side_spec)
        args.append(side)
    if out_mode == "plain":
        out_spec = pl.BlockSpec((tm, tn), lambda j, i: (i, j))
        out_shape = jax.ShapeDtypeStruct((m, n), out_dtype)
    else:
        assert w_t and n_pairs == 1 and res is None
        nmb = seq // tm
        rows = LANES + (ONES_ROWS if out_mode == "slabs_t_ones" else 0)
        out_spec = pl.BlockSpec((1, tn // LANES, rows, tm), lambda j, i: (i // nmb, j, 0, i % nmb))
        out_shape = jax.ShapeDtypeStruct((m // seq, n // LANES, rows, seq), out_dtype)
    if side is not None:
        out_spec, out_shape = [out_spec, side_spec], [out_shape, side_shape]
    return pl.pallas_call(
        functools.partial(_matmul_kernel, n_pairs=n_pairs, has_scale=scale is not None,
                          has_res=res is not None, has_side=side is not None,
                          cast_w=cast_w, w_t=w_t, out_mode=out_mode),
        grid=(n // tn, m // tm),
        in_specs=in_specs,
        out_specs=out_spec,
        out_shape=out_shape,
        scratch_shapes=scratch,
        compiler_params=_cparams(("parallel", "arbitrary")),
        name=name,
    )(*args)


def _side_specs(w, steps, step_of):
    rows, cols = w.shape
    slab = rows // steps
    assert slab * steps == rows and slab % BF16_SUBLANES == 0, (rows, steps)
    spec = pl.BlockSpec((slab, cols), lambda *g: (step_of(*g), 0))
    return spec, jax.ShapeDtypeStruct((rows, cols), BF16)


def _swiglu_up_kernel(h_ref, wg_ref, wu_ref, wd_ref, o_ref, wdb_ref):
    wdb_ref[...] = wd_ref[...].astype(BF16)
    h = h_ref[...]
    g = jnp.dot(h, wg_ref[...], preferred_element_type=F32)
    u = jnp.dot(h, wu_ref[...], preferred_element_type=F32)
    o_ref[...] = (g * jax.nn.sigmoid(g) * u).astype(o_ref.dtype)


def swiglu_up(h, wg, wu, wd, tm=2048, tn=256):
    m, k = h.shape
    n = wg.shape[1]
    tm = min(tm, m)
    tn = min(tn, n)
    nnb = n // tn
    side_spec, side_shape = _side_specs(wd, (m // tm) * nnb, lambda i, j: i * nnb + j)
    return pl.pallas_call(
        _swiglu_up_kernel,
        grid=(m // tm, nnb),
        in_specs=[pl.BlockSpec((tm, k), lambda i, j: (i, 0)),
                  pl.BlockSpec((k, tn), lambda i, j: (0, j)),
                  pl.BlockSpec((k, tn), lambda i, j: (0, j)),
                  side_spec],
        out_specs=[pl.BlockSpec((tm, tn), lambda i, j: (i, j)), side_spec],
        out_shape=[jax.ShapeDtypeStruct((m, n), BF16), side_shape],
        compiler_params=_cparams(("parallel", "parallel")),
        name="swiglu_up",
    )(h, wg, wu, wd)


def _matmul_k_kernel(a_ref, b_ref, r_ref, o_ref, acc_ref):
    kk = pl.program_id(2)

    @pl.when(kk == 0)
    def _():
        acc_ref[...] = r_ref[...]

    acc_ref[...] += jnp.dot(a_ref[...], b_ref[...], preferred_element_type=F32)

    @pl.when(kk == pl.num_programs(2) - 1)
    def _():
        o_ref[...] = acc_ref[...].astype(o_ref.dtype)


def matmul_k(a, b, res, out_dtype, tm=1024, tn=512, tk=5504, name="matmul_k"):
    m, k = a.shape
    n = b.shape[1]
    tm = min(tm, m)
    tn = min(tn, n)
    tk = min(tk, k)
    return pl.pallas_call(
        _matmul_k_kernel,
        grid=(m // tm, n // tn, k // tk),
        in_specs=[pl.BlockSpec((tm, tk), lambda i, j, kk: (i, kk)),
                  pl.BlockSpec((tk, tn), lambda i, j, kk: (kk, j)),
                  pl.BlockSpec((tm, tn), lambda i, j, kk: (i, j))],
        out_specs=pl.BlockSpec((tm, tn), lambda i, j, kk: (i, j)),
        out_shape=jax.ShapeDtypeStruct((m, n), out_dtype),
        scratch_shapes=[pltpu.VMEM((tm, tn), F32)],
        compiler_params=_cparams(("parallel", "parallel", "arbitrary")),
        name=name,
    )(a, b, res)


def _select_kernel(q_ref, kw_ref, kwq_ref, sa_ref, sb_ref, bias_ref, sa16_ref, sb16_ref,
                   s_ref, ke_ref, ko_ref, *, tq, tk, n_chunks, topk):
    sa16_ref[...] = sa_ref[...].astype(BF16)
    sb16_ref[...] = sb_ref[...].astype(BF16)
    i = pl.program_id(1)

    @pl.when(i == 0)
    def _():
        lane = lax.broadcasted_iota(jnp.int32, (tk, LANES), 1)
        for c in range(n_chunks):
            kt = kw_ref[0, 0, :, c * tk:(c + 1) * tk].T
            ke_ref[c * tk:(c + 1) * tk, :] = jnp.where(lane < IDX_DIM, kt, 0.0).astype(BF16)
            ko_ref[c * tk:(c + 1) * tk, :] = jnp.where(
                lane >= IDX_DIM, pltpu.roll(kt, IDX_DIM, axis=1), 0.0).astype(BF16)

    t0 = i * tq
    nj = (t0 + tq + tk - 1) // tk
    qpos = t0 + lax.broadcasted_iota(jnp.int32, (1, tq), 1)
    w = kwq_ref[0, 0, IDX_DIM:IDX_DIM + IDX_HEADS, :]

    def score_chunk(j, carry):
        rmin, rmax = carry
        r0 = pl.multiple_of(j * tk, tk)
        ke = ke_ref[pl.ds(r0, tk), :]
        ko = ko_ref[pl.ds(r0, tk), :]
        acc = jnp.zeros((tk, tq), F32)
        for p in range(IDX_HEADS // 2):
            qp = q_ref[0, p]
            le = jnp.dot(ke, qp, preferred_element_type=F32)
            lo = jnp.dot(ko, qp, preferred_element_type=F32)
            acc = (acc + w[2 * p:2 * p + 1, :] * jnp.maximum(le, 0.0)
                   + w[2 * p + 1:2 * p + 2, :] * jnp.maximum(lo, 0.0))
        kpos = j * tk + lax.broadcasted_iota(jnp.int32, (tk, 1), 0)
        causal = kpos <= qpos
        s_ref[j] = jnp.where(causal, acc, NEG)
        rmax = jnp.maximum(rmax, jnp.max(jnp.where(causal, acc, NEG), axis=0, keepdims=True))
        rmin = jnp.minimum(rmin, jnp.min(jnp.where(causal, acc, BIG), axis=0, keepdims=True))
        return rmin, rmax

    rmin, rmax = lax.fori_loop(
        0, nj, score_chunk,
        (jnp.full((1, tq), BIG, F32), jnp.full((1, tq), NEG, F32)))

    n_causal = (qpos + 1).astype(F32)
    keff = jnp.minimum(n_causal, float(topk))
    nacc = 64

    def count_ge(thr):
        def body(j, c):
            part = jnp.where(s_ref[j] >= thr, 1.0, 0.0)
            return c + jnp.sum(part.reshape(tk // nacc, nacc, tq), axis=0)
        c = lax.fori_loop(0, nj, body, jnp.zeros((nacc, tq), F32))
        return jnp.sum(c, axis=0, keepdims=True)

    hi0 = rmax
    c_max = count_ge(hi0)
    at_max = c_max >= keff
    lo0 = jnp.where(at_max, hi0, rmin)
    c_lo0 = jnp.where(at_max, c_max, n_causal)

    def cond(st):
        it, _, _, c_lo = st
        return jnp.logical_and(it < BISECT_MAX_ITERS, jnp.max(jnp.abs(c_lo - keff)) > 0.0)

    def halve(lo, hi, c_lo):
        mid = 0.5 * (lo + hi)
        c = count_ge(mid)
        ge = c >= keff
        return jnp.where(ge, mid, lo), jnp.where(ge, hi, mid), jnp.where(ge, c, c_lo)

    def bisect(st):
        it, lo, hi, c_lo = st
        lo, hi, c_lo = halve(*halve(lo, hi, c_lo))
        return it + 2, lo, hi, c_lo

    _, lo, _, _ = lax.while_loop(cond, bisect, (jnp.int32(0), lo0, hi0, c_lo0))

    def write_sel(j, c):
        r0 = pl.multiple_of(j * tk, tk)
        bias_ref[0, pl.ds(r0, tk), :] = jnp.where(s_ref[j] >= lo, 0.0, NEG).astype(bias_ref.dtype)
        return c

    def write_none(j, c):
        r0 = pl.multiple_of(j * tk, tk)
        bias_ref[0, pl.ds(r0, tk), :] = jnp.full((tk, tq), NEG, bias_ref.dtype)
        return c

    lax.fori_loop(0, nj, write_sel, 0)
    lax.fori_loop(nj, n_chunks, write_none, 0)


def select_mask(qt, kw, side_a, side_b, *, tq, tk, topk):
    bsz, npair, _, seq = qt.shape
    n_chunks = seq // tk
    nqb = seq // tq
    step_of = lambda b, i: b * nqb + i
    spec_a, shape_a = _side_specs(side_a, bsz * nqb, step_of)
    spec_b, shape_b = _side_specs(side_b, bsz * nqb, step_of)
    return pl.pallas_call(
        functools.partial(_select_kernel, tq=tq, tk=tk, n_chunks=n_chunks, topk=topk),
        grid=(bsz, nqb),
        in_specs=[pl.BlockSpec((1, npair, LANES, tq), lambda b, i: (b, 0, 0, i)),
                  pl.BlockSpec((1, 1, LANES, seq), lambda b, i: (b, 0, 0, 0)),
                  pl.BlockSpec((1, 1, LANES, tq), lambda b, i: (b, 0, 0, i)),
                  spec_a, spec_b],
        out_specs=[pl.BlockSpec((1, seq, tq), lambda b, i: (b, 0, i)), spec_a, spec_b],
        out_shape=[jax.ShapeDtypeStruct((bsz, seq, seq), BF16), shape_a, shape_b],
        scratch_shapes=[pltpu.VMEM((n_chunks, tk, tq), F32),
                        pltpu.VMEM((seq, LANES), BF16),
                        pltpu.VMEM((seq, LANES), BF16)],
        compiler_params=_cparams(("parallel", "arbitrary")),
        name="select_mask",
    )(qt, kw, kw, side_a, side_b)


def _attn_kernel(q_ref, k_ref, v_ref, b_ref, o_ref, m_ref, acc_ref, *, tq, tk, nh, dh):
    i = pl.program_id(1)
    j = pl.program_id(2)

    @pl.when(j == 0)
    def _():
        m_ref[...] = jnp.full(m_ref.shape, -jnp.inf, F32)
        acc_ref[...] = jnp.zeros(acc_ref.shape, F32)

    @pl.when(j * tk <= i * tq + tq - 1)
    def _():
        bias = b_ref[0]
        for h in range(nh):
            s = jnp.dot(k_ref[:, h * dh:(h + 1) * dh], q_ref[0, h], preferred_element_type=F32)
            sb = s.astype(BF16) + bias
            part = jnp.max(sb.reshape(tk // BF16_SUBLANES, BF16_SUBLANES, tq), axis=0)
            m_blk = jnp.max(part.astype(F32), axis=0, keepdims=True)
            m_old = m_ref[h]
            m_new = jnp.maximum(m_old, m_blk)
            p = jnp.exp2(sb - m_new[:1].astype(BF16))
            alpha = jnp.exp2(m_old - m_new)
            pv = jnp.dot(v_ref[0, h], p, preferred_element_type=F32)
            acc_ref[h] = alpha[:1] * acc_ref[h] + pv
            m_ref[h] = m_new

    @pl.when(j == pl.num_programs(2) - 1)
    def _():
        for h in range(nh):
            acc = acc_ref[h]
            o = acc[:dh] / acc[dh:dh + 1]
            o_ref[:, h * dh:(h + 1) * dh] = o.T.astype(o_ref.dtype)


def masked_attention(qt, kx, vt, bias, *, bsz, seq, tq, tk):
    nh, dh = ATT_HEADS, HEAD_DIM
    d_att = nh * dh
    nqb, nkb = seq // tq, seq // tk
    vrows = dh + ONES_ROWS

    def jc(i, j):
        return jnp.minimum(j, (i * tq + tq - 1) // tk)

    return pl.pallas_call(
        functools.partial(_attn_kernel, tq=tq, tk=tk, nh=nh, dh=dh),
        grid=(bsz, nqb, nkb),
        in_specs=[pl.BlockSpec((1, nh, dh, tq), lambda b, i, j: (b, 0, 0, i)),
                  pl.BlockSpec((tk, d_att), lambda b, i, j: (b * nkb + jc(i, j), 0)),
                  pl.BlockSpec((1, nh, vrows, tk), lambda b, i, j: (b, 0, 0, jc(i, j))),
                  pl.BlockSpec((1, tk, tq), lambda b, i, j: (b, jc(i, j), i))],
        out_specs=pl.BlockSpec((tq, d_att), lambda b, i, j: (b * nqb + i, 0)),
        out_shape=jax.ShapeDtypeStruct((bsz * seq, d_att), BF16),
        scratch_shapes=[pltpu.VMEM((nh, F32_SUBLANES, tq), F32),
                        pltpu.VMEM((nh, vrows, tq), F32)],
        compiler_params=_cparams(("parallel", "parallel", "arbitrary")),
        name="masked_attention",
    )(qt, kx, vt, bias)


def _gelu_tanh(x):
    return 0.5 * x * (1.0 + jnp.tanh(0.7978845608028654 * (x + 0.044715 * (x * x * x))))


def _gmlp_kernel(u_ref, v_ref, gain_ref, w_ref, bias_ref, o_ref, *, n_sub):
    tri = (lax.broadcasted_iota(jnp.int32, (CHUNK, CHUNK), 1)
           <= lax.broadcasted_iota(jnp.int32, (CHUNK, CHUNK), 0))
    for g in range(GMLP_GROUPS):
        gs = slice(g * GMLP_CH, (g + 1) * GMLP_CH)
        wg = jnp.where(tri, w_ref[g], 0.0).astype(BF16)
        gain = gain_ref[:, gs]
        bias = bias_ref[:, gs]
        for c in range(n_sub):
            rs = slice(c * CHUNK, (c + 1) * CHUNK)
            u = _gelu_tanh(u_ref[rs, gs].astype(F32))
            v = _gelu_tanh(v_ref[rs, gs].astype(F32))
            mu = jnp.mean(v, axis=-1, keepdims=True)
            vc = v - mu
            var = jnp.mean(vc * vc, axis=-1, keepdims=True)
            vn = (vc * lax.rsqrt(var + EPS) * gain).astype(BF16)
            z = jnp.dot(wg, vn, preferred_element_type=F32) + bias
            o_ref[rs, gs] = (u * z).astype(o_ref.dtype)


def gmlp_gating(uv, v_gain, w_s, b_s, tm=512):
    m = uv.shape[0]
    d = GMLP_GROUPS * GMLP_CH
    tm = min(tm, m)
    gain = v_gain.reshape(1, d).astype(F32)
    bias_full = jnp.repeat(b_s.T.astype(F32), GMLP_CH, axis=1)
    return pl.pallas_call(
        functools.partial(_gmlp_kernel, n_sub=tm // CHUNK),
        grid=(m // tm,),
        in_specs=[pl.BlockSpec((tm, d), lambda i: (i, 0)),
                  pl.BlockSpec((tm, d), lambda i: (i, 1)),
                  pl.BlockSpec((1, d), lambda i: (0, 0)),
                  pl.BlockSpec((GMLP_GROUPS, CHUNK, CHUNK), lambda i: (0, 0, 0)),
                  pl.BlockSpec((CHUNK, d), lambda i: (0, 0))],
        out_specs=pl.BlockSpec((tm, d), lambda i: (i, 0)),
        out_shape=jax.ShapeDtypeStruct((m, d), BF16),
        compiler_params=_cparams(("parallel",)),
        name="gmlp_gating",
    )(uv, uv, gain, w_s.astype(F32), bias_full)


def dsa_attention(qt, kx, vt, qit, kw, side_a, side_b, *, bsz, seq,
                  tq_sel=256, tq_att=1024, tk_sel=512, tk_att=1024):
    tq_sel, tk_sel = min(tq_sel, seq), min(tk_sel, seq)
    tq_att, tk_att = min(tq_att, seq), min(tk_att, seq)
    topk = min(TOPK_MAX, seq // 4)
    bias, side_a16, side_b16 = select_mask(qit, kw, side_a, side_b,
                                           tq=tq_sel, tk=tk_sel, topk=topk)
    att = masked_attention(qt, kx, vt, bias, bsz=bsz, seq=seq, tq=tq_att, tk=tk_att)
    return att, side_a16, side_b16


def _in_proj_scale(d_att, n_idx, d_gm):
    return jnp.concatenate([
        jnp.full((d_att,), LOG2E * HEAD_DIM ** -0.5, F32),
        jnp.ones((2 * d_att,), F32),
        jnp.full((n_idx,), IDX_DIM ** -0.5, F32),
        jnp.ones((IDX_DIM,), F32),
        jnp.full((IDX_HEADS,), IDX_HEADS ** -0.5, F32),
        jnp.ones((2 * d_gm,), F32)]).reshape(-1, 1)


def kernel(x, norm_mix, w_in, gmlp_v_gain, w_spatial, b_spatial, w_out, norm_ffn,
           w_gate, w_up, w_down, norm_final):
    bsz, seq, d_model = x.shape
    m = bsz * seq
    d_att = ATT_HEADS * HEAD_DIM
    d_gm = GMLP_GROUPS * GMLP_CH
    n_idx = IDX_HEADS * IDX_DIM
    depth = w_in.shape[0]
    tn = 512
    scale = _in_proj_scale(d_att, n_idx, d_gm)
    xf = x.reshape(m, d_model)
    for i in range(depth):
        wt = jnp.swapaxes(w_in[i], 0, 1)
        uv_off = 3 * d_att + n_idx + IDX_DIM + IDX_HEADS
        w_uv = wt[uv_off:].astype(BF16)

        h = rmsnorm(xf, norm_mix[i], BF16)
        qt = matmul([h], [(wt, 0)], BF16, n=d_att, col_off=0, w_t=True, scale=scale,
                    out_mode="slabs_t", seq=seq, tn=tn, name="proj_q")
        kx = matmul([h], [(wt, 0)], BF16, n=d_att, col_off=d_att // tn, w_t=True, scale=scale,
                    tn=tn, name="proj_k")
        vt = matmul([h], [(wt, 0)], BF16, n=d_att, col_off=2 * d_att // tn, w_t=True, scale=scale,
                    out_mode="slabs_t_ones", seq=seq, tn=tn, name="proj_v")
        qit = matmul([h], [(wt, 0)], BF16, n=n_idx, col_off=3 * d_att // tn, w_t=True, scale=scale,
                     out_mode="slabs_t", seq=seq, tn=tn, name="proj_qidx")
        kw = matmul([h], [(wt, 0)], F32, n=LANES, col_off=(3 * d_att + n_idx) // LANES,
                    w_t=True, scale=scale, out_mode="slabs_t", seq=seq, tn=LANES, name="proj_small")
        proj_uv, wo = matmul([h], [(w_uv, 0)], BF16, n=2 * d_gm, w_t=True, side=w_out[i],
                             tn=1024, name="proj_uv")

        att, wg, wu = dsa_attention(qt, kx, vt, qit, kw, w_gate[i], w_up[i], bsz=bsz, seq=seq)
        gm = gmlp_gating(proj_uv, gmlp_v_gain[i], w_spatial[i], b_spatial[i])

        xf = matmul([att, gm], [(wo, 0), (wo, 1)], F32, n=d_model, res=xf, tn=1024, name="out_proj")

        h2 = rmsnorm(xf, norm_ffn[i], BF16)
        hid, wd = swiglu_up(h2, wg, wu, w_down[i])
        xf = matmul_k(hid, wd, xf, F32, name="ffn_down")
    out = rmsnorm(xf, norm_final, x.dtype)
    return out.reshape(bsz, seq, d_model)
```

```python
import functools

import jax
import jax.numpy as jnp
from jax import lax
from jax.experimental import pallas as pl
from jax.experimental.pallas import tpu as pltpu

F32 = jnp.float32
BF16 = jnp.bfloat16

ATT_HEADS = 16
HEAD_DIM = 128
IDX_HEADS = 32
IDX_DIM = 64
TOPK_MAX = 256
GMLP_GROUPS = 16
GMLP_CH = 128
CHUNK = 128
EPS = 1e-6
NEG = -1e30
BIG = 1e30
LOG2E = 1.4426950408889634

LANES = 128
F32_SUBLANES = 8
BF16_SUBLANES = 16
ONES_ROWS = BF16_SUBLANES
VMEM_LIMIT = 56 * 1024 * 1024
BISECT_MAX_ITERS = 48


def _cparams(sem):
    return pltpu.CompilerParams(dimension_semantics=sem, vmem_limit_bytes=VMEM_LIMIT)


def _rmsnorm_kernel(*refs, has_add):
    x_ref, g_ref, o_ref = refs[0], refs[-2], refs[-1]
    x = x_ref[...].astype(F32)
    if has_add:
        x = x + refs[1][...].astype(F32)
    ms = jnp.mean(x * x, axis=-1, keepdims=True)
    o_ref[...] = (x * lax.rsqrt(ms + EPS) * g_ref[...]).astype(o_ref.dtype)


def rmsnorm(x, g, out_dtype, add=None, tm=512):
    m, d = x.shape
    tm = min(tm, m)
    row_spec = pl.BlockSpec((tm, d), lambda i: (i, 0))
    ins = [x] if add is None else [x, add]
    return pl.pallas_call(
        functools.partial(_rmsnorm_kernel, has_add=add is not None),
        grid=(m // tm,),
        in_specs=[row_spec] * len(ins) + [pl.BlockSpec((1, d), lambda i: (0, 0))],
        out_specs=row_spec,
        out_shape=jax.ShapeDtypeStruct((m, d), out_dtype),
        compiler_params=_cparams(("parallel",)),
        name="rmsnorm",
    )(*ins, g.reshape(1, d).astype(F32))


def _matmul_kernel(*refs, n_pairs, has_scale, has_res, has_side, cast_w, w_t, out_mode):
    a_refs = refs[:n_pairs]
    w_refs = refs[n_pairs:2 * n_pairs]
    pos = 2 * n_pairs
    sc_ref = refs[pos] if has_scale else None
    pos += int(has_scale)
    r_ref = refs[pos] if has_res else None
    pos += int(has_res)
    side_ref = refs[pos] if has_side else None
    pos += int(has_side)
    o_ref = refs[pos]
    pos += 1
    if has_side:
        refs[pos][...] = side_ref[...].astype(BF16)
        pos += 1
    wb_refs = refs[pos:]

    if cast_w:
        @pl.when(pl.program_id(1) == 0)
        def _():
            for p in range(n_pairs):
                w = w_refs[p][...]
                if has_scale:
                    w = w * sc_ref[...]
                wb_refs[p][...] = w.astype(BF16)
        rhs = wb_refs
    else:
        rhs = w_refs

    nt = (((1,), (1,)), ((), ()))
    if out_mode in ("slabs_t", "slabs_t_ones"):
        acc = lax.dot_general(rhs[0][...], a_refs[0][...], nt, preferred_element_type=F32)
        for pp in range(acc.shape[0] // LANES):
            o_ref[0, pp, :LANES, :] = acc[pp * LANES:(pp + 1) * LANES, :].astype(o_ref.dtype)
            if out_mode == "slabs_t_ones":
                o_ref[0, pp, LANES:, :] = jnp.ones((ONES_ROWS, acc.shape[1]), o_ref.dtype)
        return

    def mm(p):
        if w_t:
            return lax.dot_general(a_refs[p][...], rhs[p][...], nt, preferred_element_type=F32)
        return jnp.dot(a_refs[p][...], rhs[p][...], preferred_element_type=F32)

    acc = mm(0)
    for p in range(1, n_pairs):
        acc = acc + mm(p)
    if has_res:
        acc = acc + r_ref[...]

    o_ref[...] = acc.astype(o_ref.dtype)


def matmul(a_list, w_list, out_dtype, *, n, col_off=0, w_t=False, scale=None, res=None,
           side=None, out_mode="plain", seq=None, tm=1024, tn=512, name="matmul"):
    n_pairs = len(a_list)
    m = a_list[0].shape[0]
    tm = min(tm, m)
    tn = min(tn, n)
    cast_w = w_list[0][0].dtype != BF16
    in_specs, args, scratch = [], [], []
    for a in a_list:
        in_specs.append(pl.BlockSpec((tm, a.shape[1]), lambda j, i: (i, 0)))
        args.append(a)
    for a, (w, rb) in zip(a_list, w_list):
        k = a.shape[1]
        if w_t:
            in_specs.append(pl.BlockSpec((tn, k), lambda j, i: (j + col_off, 0)))
        else:
            in_specs.append(pl.BlockSpec((k, tn), lambda j, i, rb=rb: (rb, j + col_off)))
        args.append(w)
        if cast_w:
            scratch.append(pltpu.VMEM((tn, k) if w_t else (k, tn), BF16))
    if scale is not None:
        if w_t:
            in_specs.append(pl.BlockSpec((tn, 1), lambda j, i: (j + col_off, 0)))
        else:
            in_specs.append(pl.BlockSpec((1, tn), lambda j, i: (0, j + col_off)))
        args.append(scale)
    if res is not None:
        in_specs.append(pl.BlockSpec((tm, tn), lambda j, i: (i, j)))
        args.append(res)
    nmb_all = m // tm
    if side is not None:
        side_spec, side_shape = _side_specs(side, (n // tn) * nmb_all, lambda j, i: j * nmb_all + i)
        in_specs.append(side_spec)
        args.append(side)
    if out_mode == "plain":
        out_spec = pl.BlockSpec((tm, tn), lambda j, i: (i, j))
        out_shape = jax.ShapeDtypeStruct((m, n), out_dtype)
    else:
        assert w_t and n_pairs == 1 and res is None
        nmb = seq // tm
        rows = LANES + (ONES_ROWS if out_mode == "slabs_t_ones" else 0)
        out_spec = pl.BlockSpec((1, tn // LANES, rows, tm), lambda j, i: (i // nmb, j, 0, i % nmb))
        out_shape = jax.ShapeDtypeStruct((m // seq, n // LANES, rows, seq), out_dtype)
    if side is not None:
        out_spec, out_shape = [out_spec, side_spec], [out_shape, side_shape]
    return pl.pallas_call(
        functools.partial(_matmul_kernel, n_pairs=n_pairs, has_scale=scale is not None,
                          has_res=res is not None, has_side=side is not None,
                          cast_w=cast_w, w_t=w_t, out_mode=out_mode),
        grid=(n // tn, m // tm),
        in_specs=in_specs,
        out_specs=out_spec,
        out_shape=out_shape,
        scratch_shapes=scratch,
        compiler_params=_cparams(("parallel", "arbitrary")),
        name=name,
    )(*args)


def _side_specs(w, steps, step_of):
    rows, cols = w.shape
    slab = rows // steps
    assert slab * steps == rows and slab % BF16_SUBLANES == 0, (rows, steps)
    spec = pl.BlockSpec((slab, cols), lambda *g: (step_of(*g), 0))
    return spec, jax.ShapeDtypeStruct((rows, cols), BF16)


def _swiglu_up_kernel(h_ref, wg_ref, wu_ref, wd_ref, o_ref, wdb_ref):
    wdb_ref[...] = wd_ref[...].astype(BF16)
    h = h_ref[...]
    g = jnp.dot(h, wg_ref[...], preferred_element_type=F32)
    u = jnp.dot(h, wu_ref[...], preferred_element_type=F32)
    o_ref[...] = (g * jax.nn.sigmoid(g) * u).astype(o_ref.dtype)


def swiglu_up(h, wg, wu, wd, tm=2048, tn=256):
    m, k = h.shape
    n = wg.shape[1]
    tm = min(tm, m)
    tn = min(tn, n)
    nnb = n // tn
    side_spec, side_shape = _side_specs(wd, (m // tm) * nnb, lambda i, j: i * nnb + j)
    return pl.pallas_call(
        _swiglu_up_kernel,
        grid=(m // tm, nnb),
        in_specs=[pl.BlockSpec((tm, k), lambda i, j: (i, 0)),
                  pl.BlockSpec((k, tn), lambda i, j: (0, j)),
                  pl.BlockSpec((k, tn), lambda i, j: (0, j)),
                  side_spec],
        out_specs=[pl.BlockSpec((tm, tn), lambda i, j: (i, j)), side_spec],
        out_shape=[jax.ShapeDtypeStruct((m, n), BF16), side_shape],
        compiler_params=_cparams(("parallel", "parallel")),
        name="swiglu_up",
    )(h, wg, wu, wd)


def _matmul_k_kernel(*refs, has_res):
    a_ref, b_ref = refs[0], refs[1]
    o_ref, acc_ref = refs[-2], refs[-1]
    kk = pl.program_id(2)
    part = jnp.dot(a_ref[...], b_ref[...], preferred_element_type=F32)

    @pl.when(kk == 0)
    def _():
        acc_ref[...] = part + refs[2][...] if has_res else part

    @pl.when(kk > 0)
    def _():
        acc_ref[...] += part

    @pl.when(kk == pl.num_programs(2) - 1)
    def _():
        o_ref[...] = acc_ref[...].astype(o_ref.dtype)


def matmul_k(a, b, res, out_dtype, tm=1024, tn=512, tk=5504, name="matmul_k"):
    m, k = a.shape
    n = b.shape[1]
    tm = min(tm, m)
    tn = min(tn, n)
    tk = min(tk, k)
    in_specs = [pl.BlockSpec((tm, tk), lambda i, j, kk: (i, kk)),
                pl.BlockSpec((tk, tn), lambda i, j, kk: (kk, j))]
    args = [a, b]
    if res is not None:
        in_specs.append(pl.BlockSpec((tm, tn), lambda i, j, kk: (i, j)))
        args.append(res)
    return pl.pallas_call(
        functools.partial(_matmul_k_kernel, has_res=res is not None),
        grid=(m // tm, n // tn, k // tk),
        in_specs=in_specs,
        out_specs=pl.BlockSpec((tm, tn), lambda i, j, kk: (i, j)),
        out_shape=jax.ShapeDtypeStruct((m, n), out_dtype),
        scratch_shapes=[pltpu.VMEM((tm, tn), F32)],
        compiler_params=_cparams(("parallel", "parallel", "arbitrary")),
        name=name,
    )(*args)


def _select_kernel(q_ref, kw_ref, kwq_ref, sa_ref, sb_ref, bias_ref, sa16_ref, sb16_ref,
                   s_ref, ke_ref, ko_ref, *, tq, tk, n_chunks, topk):
    sa16_ref[...] = sa_ref[...].astype(BF16)
    sb16_ref[...] = sb_ref[...].astype(BF16)
    i = pl.program_id(1)

    @pl.when(i == 0)
    def _():
        lane = lax.broadcasted_iota(jnp.int32, (tk, LANES), 1)
        for c in range(n_chunks):
            kt = kw_ref[0, 0, :, c * tk:(c + 1) * tk].T
            ke_ref[c * tk:(c + 1) * tk, :] = jnp.where(lane < IDX_DIM, kt, 0.0).astype(BF16)
            ko_ref[c * tk:(c + 1) * tk, :] = jnp.where(
                lane >= IDX_DIM, pltpu.roll(kt, IDX_DIM, axis=1), 0.0).astype(BF16)

    t0 = i * tq
    nj = (t0 + tq + tk - 1) // tk
    qpos = t0 + lax.broadcasted_iota(jnp.int32, (1, tq), 1)
    w = kwq_ref[0, 0, IDX_DIM:IDX_DIM + IDX_HEADS, :]

    def score_chunk(j, carry):
        rmin, rmax = carry
        r0 = pl.multiple_of(j * tk, tk)
        ke = ke_ref[pl.ds(r0, tk), :]
        ko = ko_ref[pl.ds(r0, tk), :]
        acc = jnp.zeros((tk, tq), F32)
        for p in range(IDX_HEADS // 2):
            qp = q_ref[0, p]
            le = jnp.dot(ke, qp, preferred_element_type=F32)
            lo = jnp.dot(ko, qp, preferred_element_type=F32)
            acc = (acc + w[2 * p:2 * p + 1, :] * jnp.maximum(le, 0.0)
                   + w[2 * p + 1:2 * p + 2, :] * jnp.maximum(lo, 0.0))
        kpos = j * tk + lax.broadcasted_iota(jnp.int32, (tk, 1), 0)
        causal = kpos <= qpos
        s_ref[j] = jnp.where(causal, acc, NEG)
        rmax = jnp.maximum(rmax, jnp.max(jnp.where(causal, acc, NEG), axis=0, keepdims=True))
        rmin = jnp.minimum(rmin, jnp.min(jnp.where(causal, acc, BIG), axis=0, keepdims=True))
        return rmin, rmax

    rmin, rmax = lax.fori_loop(
        0, nj, score_chunk,
        (jnp.full((1, tq), BIG, F32), jnp.full((1, tq), NEG, F32)))

    n_causal = (qpos + 1).astype(F32)
    keff = jnp.minimum(n_causal, float(topk))
    nacc = 64

    def count_ge(thr):
        def body(j, c):
            part = jnp.where(s_ref[j] >= thr, 1.0, 0.0)
            return c + jnp.sum(part.reshape(tk // nacc, nacc, tq), axis=0)
        c = lax.fori_loop(0, nj, body, jnp.zeros((nacc, tq), F32))
        return jnp.sum(c, axis=0, keepdims=True)

    hi0 = rmax
    c_max = count_ge(hi0)
    at_max = c_max >= keff
    lo0 = jnp.where(at_max, hi0, rmin)
    c_lo0 = jnp.where(at_max, c_max, n_causal)

    def cond(st):
        it, _, _, c_lo = st
        return jnp.logical_and(it < BISECT_MAX_ITERS, jnp.max(jnp.abs(c_lo - keff)) > 0.0)

    def halve(lo, hi, c_lo):
        mid = 0.5 * (lo + hi)
        c = count_ge(mid)
        ge = c >= keff
        return jnp.where(ge, mid, lo), jnp.where(ge, hi, mid), jnp.where(ge, c, c_lo)

    def bisect(st):
        it, lo, hi, c_lo = st
        lo, hi, c_lo = halve(*halve(lo, hi, c_lo))
        return it + 2, lo, hi, c_lo

    _, lo, _, _ = lax.while_loop(cond, bisect, (jnp.int32(0), lo0, hi0, c_lo0))

    def write_sel(j, c):
        r0 = pl.multiple_of(j * tk, tk)
        bias_ref[0, pl.ds(r0, tk), :] = jnp.where(s_ref[j] >= lo, 0.0, NEG).astype(bias_ref.dtype)
        return c

    def write_none(j, c):
        r0 = pl.multiple_of(j * tk, tk)
        bias_ref[0, pl.ds(r0, tk), :] = jnp.full((tk, tq), NEG, bias_ref.dtype)
        return c

    lax.fori_loop(0, nj, write_sel, 0)
    lax.fori_loop(nj, n_chunks, write_none, 0)


def select_mask(qt, kw, side_a, side_b, *, tq, tk, topk):
    bsz, npair, _, seq = qt.shape
    n_chunks = seq // tk
    nqb = seq // tq
    step_of = lambda b, i: b * nqb + i
    spec_a, shape_a = _side_specs(side_a, bsz * nqb, step_of)
    spec_b, shape_b = _side_specs(side_b, bsz * nqb, step_of)
    return pl.pallas_call(
        functools.partial(_select_kernel, tq=tq, tk=tk, n_chunks=n_chunks, topk=topk),
        grid=(bsz, nqb),
        in_specs=[pl.BlockSpec((1, npair, LANES, tq), lambda b, i: (b, 0, 0, i)),
                  pl.BlockSpec((1, 1, LANES, seq), lambda b, i: (b, 0, 0, 0)),
                  pl.BlockSpec((1, 1, LANES, tq), lambda b, i: (b, 0, 0, i)),
                  spec_a, spec_b],
        out_specs=[pl.BlockSpec((1, seq, tq), lambda b, i: (b, 0, i)), spec_a, spec_b],
        out_shape=[jax.ShapeDtypeStruct((bsz, seq, seq), BF16), shape_a, shape_b],
        scratch_shapes=[pltpu.VMEM((n_chunks, tk, tq), F32),
                        pltpu.VMEM((seq, LANES), BF16),
                        pltpu.VMEM((seq, LANES), BF16)],
        compiler_params=_cparams(("parallel", "arbitrary")),
        name="select_mask",
    )(qt, kw, kw, side_a, side_b)


def _attn_kernel(q_ref, k_ref, v_ref, b_ref, o_ref, m_ref, acc_ref, *, tq, tk, nh, dh):
    i = pl.program_id(1)
    j = pl.program_id(2)

    @pl.when(j == 0)
    def _():
        m_ref[...] = jnp.full(m_ref.shape, -jnp.inf, F32)
        acc_ref[...] = jnp.zeros(acc_ref.shape, F32)

    @pl.when(j * tk <= i * tq + tq - 1)
    def _():
        bias = b_ref[0]
        for h in range(nh):
            s = jnp.dot(k_ref[:, h * dh:(h + 1) * dh], q_ref[0, h], preferred_element_type=F32)
            sb = s.astype(BF16) + bias
            part = jnp.max(sb.reshape(tk // BF16_SUBLANES, BF16_SUBLANES, tq), axis=0)
            m_blk = jnp.max(part.astype(F32), axis=0, keepdims=True)
            m_old = m_ref[h]
            m_new = jnp.maximum(m_old, m_blk)
            p = jnp.exp2(sb - m_new[:1].astype(BF16))
            alpha = jnp.exp2(m_old - m_new)
            pv = jnp.dot(v_ref[0, h], p, preferred_element_type=F32)
            acc_ref[h] = alpha[:1] * acc_ref[h] + pv
            m_ref[h] = m_new

    @pl.when(j == pl.num_programs(2) - 1)
    def _():
        for h in range(nh):
            acc = acc_ref[h]
            o = acc[:dh] / acc[dh:dh + 1]
            o_ref[:, h * dh:(h + 1) * dh] = o.T.astype(o_ref.dtype)


def masked_attention(qt, kx, vt, bias, *, bsz, seq, tq, tk):
    nh, dh = ATT_HEADS, HEAD_DIM
    d_att = nh * dh
    nqb, nkb = seq // tq, seq // tk
    vrows = dh + ONES_ROWS

    def jc(i, j):
        return jnp.minimum(j, (i * tq + tq - 1) // tk)

    return pl.pallas_call(
        functools.partial(_attn_kernel, tq=tq, tk=tk, nh=nh, dh=dh),
        grid=(bsz, nqb, nkb),
        in_specs=[pl.BlockSpec((1, nh, dh, tq), lambda b, i, j: (b, 0, 0, i)),
                  pl.BlockSpec((tk, d_att), lambda b, i, j: (b * nkb + jc(i, j), 0)),
                  pl.BlockSpec((1, nh, vrows, tk), lambda b, i, j: (b, 0, 0, jc(i, j))),
                  pl.BlockSpec((1, tk, tq), lambda b, i, j: (b, jc(i, j), i))],
        out_specs=pl.BlockSpec((tq, d_att), lambda b, i, j: (b * nqb + i, 0)),
        out_shape=jax.ShapeDtypeStruct((bsz * seq, d_att), BF16),
        scratch_shapes=[pltpu.VMEM((nh, F32_SUBLANES, tq), F32),
                        pltpu.VMEM((nh, vrows, tq), F32)],
        compiler_params=_cparams(("parallel", "parallel", "arbitrary")),
        name="masked_attention",
    )(qt, kx, vt, bias)


def _gelu_tanh(x):
    return 0.5 * x * (1.0 + jnp.tanh(0.7978845608028654 * (x + 0.044715 * (x * x * x))))


def _gmlp_kernel(h_ref, wu_ref, wv_ref, gain_ref, w_ref, bias_ref, side_ref, o_ref, side16_ref,
                 *, n_sub, n_grp):
    side16_ref[...] = side_ref[...].astype(BF16)
    nt = (((1,), (1,)), ((), ()))
    h = h_ref[...]
    u_all = lax.dot_general(h, wu_ref[...], nt, preferred_element_type=F32)
    v_all = lax.dot_general(h, wv_ref[...], nt, preferred_element_type=F32)
    tri = (lax.broadcasted_iota(jnp.int32, (CHUNK, CHUNK), 1)
           <= lax.broadcasted_iota(jnp.int32, (CHUNK, CHUNK), 0))
    for g in range(n_grp):
        gs = slice(g * GMLP_CH, (g + 1) * GMLP_CH)
        wg = jnp.where(tri, w_ref[g], 0.0).astype(BF16)
        gain = gain_ref[:, gs]
        bias = bias_ref[:, gs]
        for c in range(n_sub):
            rs = slice(c * CHUNK, (c + 1) * CHUNK)
            u = _gelu_tanh(u_all[rs, gs])
            v = _gelu_tanh(v_all[rs, gs])
            mu = jnp.mean(v, axis=-1, keepdims=True)
            vc = v - mu
            var = jnp.mean(vc * vc, axis=-1, keepdims=True)
            vn = (vc * lax.rsqrt(var + EPS) * gain).astype(BF16)
            z = jnp.dot(wg, vn, preferred_element_type=F32) + bias
            o_ref[rs, gs] = (u * z).astype(o_ref.dtype)


def gmlp_mixer(h, w_uv_t, v_gain, w_s, b_s, side, tm=1024, n_grp=4):
    m, k = h.shape
    d = GMLP_GROUPS * GMLP_CH
    tm = min(tm, m)
    tn = n_grp * GMLP_CH
    ngb = d // tn
    nmb = m // tm
    gain = v_gain.reshape(1, d).astype(F32)
    bias_full = jnp.repeat(b_s.T.astype(F32), GMLP_CH, axis=1)
    side_spec, side_shape = _side_specs(side, ngb * nmb, lambda j, i: j * nmb + i)
    return pl.pallas_call(
        functools.partial(_gmlp_kernel, n_sub=tm // CHUNK, n_grp=n_grp),
        grid=(ngb, nmb),
        in_specs=[pl.BlockSpec((tm, k), lambda j, i: (i, 0)),
                  pl.BlockSpec((tn, k), lambda j, i: (j, 0)),
                  pl.BlockSpec((tn, k), lambda j, i: (j + ngb, 0)),
                  pl.BlockSpec((1, tn), lambda j, i: (0, j)),
                  pl.BlockSpec((n_grp, CHUNK, CHUNK), lambda j, i: (j, 0, 0)),
                  pl.BlockSpec((CHUNK, tn), lambda j, i: (0, j)),
                  side_spec],
        out_specs=[pl.BlockSpec((tm, tn), lambda j, i: (i, j)), side_spec],
        out_shape=[jax.ShapeDtypeStruct((m, d), BF16), side_shape],
        compiler_params=_cparams(("parallel", "parallel")),
        name="gmlp_mixer",
    )(h, w_uv_t, w_uv_t, gain, w_s.astype(F32), bias_full, side)


def dsa_attention(qt, kx, vt, qit, kw, side_a, side_b, *, bsz, seq,
                  tq_sel=256, tq_att=1024, tk_sel=512, tk_att=1024):
    tq_sel, tk_sel = min(tq_sel, seq), min(tk_sel, seq)
    tq_att, tk_att = min(tq_att, seq), min(tk_att, seq)
    topk = min(TOPK_MAX, seq // 4)
    bias, side_a16, side_b16 = select_mask(qit, kw, side_a, side_b,
                                           tq=tq_sel, tk=tk_sel, topk=topk)
    att = masked_attention(qt, kx, vt, bias, bsz=bsz, seq=seq, tq=tq_att, tk=tk_att)
    return att, side_a16, side_b16


def _in_proj_scale(d_att, n_idx, d_gm):
    return jnp.concatenate([
        jnp.full((d_att,), LOG2E * HEAD_DIM ** -0.5, F32),
        jnp.ones((2 * d_att,), F32),
        jnp.full((n_idx,), IDX_DIM ** -0.5, F32),
        jnp.ones((IDX_DIM,), F32),
        jnp.full((IDX_HEADS,), IDX_HEADS ** -0.5, F32),
        jnp.ones((2 * d_gm,), F32)]).reshape(-1, 1)


def kernel(x, norm_mix, w_in, gmlp_v_gain, w_spatial, b_spatial, w_out, norm_ffn,
           w_gate, w_up, w_down, norm_final):
    bsz, seq, d_model = x.shape
    m = bsz * seq
    d_att = ATT_HEADS * HEAD_DIM
    d_gm = GMLP_GROUPS * GMLP_CH
    n_idx = IDX_HEADS * IDX_DIM
    depth = w_in.shape[0]
    tn = 512
    scale = _in_proj_scale(d_att, n_idx, d_gm)
    xf = x.reshape(m, d_model)
    for i in range(depth):
        wt = jnp.swapaxes(w_in[i], 0, 1)
        uv_off = 3 * d_att + n_idx + IDX_DIM + IDX_HEADS
        w_uv = wt[uv_off:].astype(BF16)

        h = rmsnorm(xf, norm_mix[i], BF16)
        qt = matmul([h], [(wt, 0)], BF16, n=d_att, col_off=0, w_t=True, scale=scale,
                    out_mode="slabs_t", seq=seq, tn=tn, name="proj_q")
        kx = matmul([h], [(wt, 0)], BF16, n=d_att, col_off=d_att // tn, w_t=True, scale=scale,
                    tn=tn, name="proj_k")
        vt = matmul([h], [(wt, 0)], BF16, n=d_att, col_off=2 * d_att // tn, w_t=True, scale=scale,
                    out_mode="slabs_t_ones", seq=seq, tn=tn, name="proj_v")
        qit = matmul([h], [(wt, 0)], BF16, n=n_idx, col_off=3 * d_att // tn, w_t=True, scale=scale,
                     out_mode="slabs_t", seq=seq, tn=tn, name="proj_qidx")
        kw = matmul([h], [(wt, 0)], F32, n=LANES, col_off=(3 * d_att + n_idx) // LANES,
                    w_t=True, scale=scale, out_mode="slabs_t", seq=seq, tn=LANES, name="proj_small")
        gm, wo = gmlp_mixer(h, w_uv, gmlp_v_gain[i], w_spatial[i], b_spatial[i], w_out[i])

        att, wg, wu = dsa_attention(qt, kx, vt, qit, kw, w_gate[i], w_up[i], bsz=bsz, seq=seq)

        xf = matmul([att, gm], [(wo, 0), (wo, 1)], F32, n=d_model, res=xf, tn=1024, name="out_proj")

        h2 = rmsnorm(xf, norm_ffn[i], BF16)
        hid, wd = swiglu_up(h2, wg, wu, w_down[i])
        if i + 1 < depth:
            xf = matmul_k(hid, wd, xf, F32, name="ffn_down")
    y = matmul_k(hid, wd, None, BF16, tn=1024, name="ffn_down")
    out = rmsnorm(xf, norm_final, x.dtype, add=y)
    return out.reshape(bsz, seq, d_model)
```

```python
import functools

import jax
import jax.numpy as jnp
from jax import lax
from jax.experimental import pallas as pl
from jax.experimental.pallas import tpu as pltpu

F32 = jnp.float32
BF16 = jnp.bfloat16

ATT_HEADS = 16
HEAD_DIM = 128
IDX_HEADS = 32
IDX_DIM = 64
TOPK_MAX = 256
GMLP_GROUPS = 16
GMLP_CH = 128
CHUNK = 128
EPS = 1e-6
NEG = -1e30
BIG = 1e30
LOG2E = 1.4426950408889634

LANES = 128
F32_SUBLANES = 8
BF16_SUBLANES = 16
ONES_ROWS = BF16_SUBLANES
VMEM_LIMIT = 56 * 1024 * 1024
BISECT_MAX_ITERS = 320


def _cparams(sem):
    return pltpu.CompilerParams(dimension_semantics=sem, vmem_limit_bytes=VMEM_LIMIT)


def _rmsnorm_kernel(*refs, has_add):
    x_ref, g_ref, o_ref = refs[0], refs[-2], refs[-1]
    x = x_ref[...].astype(F32)
    if has_add:
        x = x + refs[1][...].astype(F32)
    ms = jnp.mean(x * x, axis=-1, keepdims=True)
    o_ref[...] = (x * lax.rsqrt(ms + EPS) * g_ref[...]).astype(o_ref.dtype)


def rmsnorm(x, g, out_dtype, add=None, tm=512):
    m, d = x.shape
    tm = min(tm, m)
    row_spec = pl.BlockSpec((tm, d), lambda i: (i, 0))
    ins = [x] if add is None else [x, add]
    return pl.pallas_call(
        functools.partial(_rmsnorm_kernel, has_add=add is not None),
        grid=(m // tm,),
        in_specs=[row_spec] * len(ins) + [pl.BlockSpec((1, d), lambda i: (0, 0))],
        out_specs=row_spec,
        out_shape=jax.ShapeDtypeStruct((m, d), out_dtype),
        compiler_params=_cparams(("parallel",)),
        name="rmsnorm",
    )(*ins, g.reshape(1, d).astype(F32))


def _matmul_kernel(*refs, n_pairs, has_scale, has_res, has_side, cast_w, w_t, out_mode,
                   post_scale):
    a_refs = refs[:n_pairs]
    w_refs = refs[n_pairs:2 * n_pairs]
    pos = 2 * n_pairs
    sc_ref = refs[pos] if has_scale else None
    pos += int(has_scale)
    r_ref = refs[pos] if has_res else None
    pos += int(has_res)
    side_ref = refs[pos] if has_side else None
    pos += int(has_side)
    o_ref = refs[pos]
    pos += 1
    if has_side:
        refs[pos][...] = side_ref[...].astype(BF16)
        pos += 1
    wb_refs = refs[pos:]

    if cast_w:
        @pl.when(pl.program_id(1) == 0)
        def _():
            for p in range(n_pairs):
                w = w_refs[p][...]
                if has_scale:
                    w = w * sc_ref[...]
                wb_refs[p][...] = w.astype(BF16)
        rhs = wb_refs
    else:
        rhs = w_refs

    nt = (((1,), (1,)), ((), ()))
    if out_mode in ("slabs_t", "slabs_t_ones"):
        acc = lax.dot_general(rhs[0][...], a_refs[0][...], nt, preferred_element_type=F32)
        if post_scale is not None:
            acc = acc * post_scale
        for pp in range(acc.shape[0] // LANES):
            o_ref[0, pp, :LANES, :] = acc[pp * LANES:(pp + 1) * LANES, :].astype(o_ref.dtype)
            if out_mode == "slabs_t_ones":
                o_ref[0, pp, LANES:, :] = jnp.ones((ONES_ROWS, acc.shape[1]), o_ref.dtype)
        return

    def mm(p):
        if w_t:
            return lax.dot_general(a_refs[p][...], rhs[p][...], nt, preferred_element_type=F32)
        return jnp.dot(a_refs[p][...], rhs[p][...], preferred_element_type=F32)

    acc = mm(0)
    for p in range(1, n_pairs):
        acc = acc + mm(p)
    if post_scale is not None:
        acc = acc * post_scale
    if has_res:
        acc = acc + r_ref[...]

    o_ref[...] = acc.astype(o_ref.dtype)


def matmul(a_list, w_list, out_dtype, *, n, col_off=0, w_t=False, scale=None, post_scale=None,
           res=None, side=None, side_rows=None, out_mode="plain", seq=None, tm=1024, tn=512,
           name="matmul"):
    n_pairs = len(a_list)
    m = a_list[0].shape[0]
    tm = min(tm, m)
    tn = min(tn, n)
    cast_w = w_list[0][0].dtype != BF16
    in_specs, args, scratch = [], [], []
    for a in a_list:
        in_specs.append(pl.BlockSpec((tm, a.shape[1]), lambda j, i: (i, 0)))
        args.append(a)
    for a, (w, rb) in zip(a_list, w_list):
        k = a.shape[1]
        if w_t:
            in_specs.append(pl.BlockSpec((tn, k), lambda j, i: (j + col_off, 0)))
        else:
            in_specs.append(pl.BlockSpec((k, tn), lambda j, i, rb=rb: (rb, j + col_off)))
        args.append(w)
        if cast_w:
            scratch.append(pltpu.VMEM((tn, k) if w_t else (k, tn), BF16))
    if scale is not None:
        if w_t:
            in_specs.append(pl.BlockSpec((tn, 1), lambda j, i: (j + col_off, 0)))
        else:
            in_specs.append(pl.BlockSpec((1, tn), lambda j, i: (0, j + col_off)))
        args.append(scale)
    if res is not None:
        in_specs.append(pl.BlockSpec((tm, tn), lambda j, i: (i, j)))
        args.append(res)
    nmb_all = m // tm
    if side is not None:
        side_spec, side_shape = _side_specs(side, (n // tn) * nmb_all,
                                            lambda j, i: j * nmb_all + i, rows=side_rows)
        in_specs.append(side_spec)
        args.append(side)
    if out_mode == "plain":
        out_spec = pl.BlockSpec((tm, tn), lambda j, i: (i, j))
        out_shape = jax.ShapeDtypeStruct((m, n), out_dtype)
    else:
        assert w_t and n_pairs == 1 and res is None
        nmb = seq // tm
        rows = LANES + (ONES_ROWS if out_mode == "slabs_t_ones" else 0)
        out_spec = pl.BlockSpec((1, tn // LANES, rows, tm), lambda j, i: (i // nmb, j, 0, i % nmb))
        out_shape = jax.ShapeDtypeStruct((m // seq, n // LANES, rows, seq), out_dtype)
    if side is not None:
        out_spec, out_shape = [out_spec, side_spec], [out_shape, side_shape]
    return pl.pallas_call(
        functools.partial(_matmul_kernel, n_pairs=n_pairs, has_scale=scale is not None,
                          has_res=res is not None, has_side=side is not None,
                          cast_w=cast_w, w_t=w_t, out_mode=out_mode, post_scale=post_scale),
        grid=(n // tn, m // tm),
        in_specs=in_specs,
        out_specs=out_spec,
        out_shape=out_shape,
        scratch_shapes=scratch,
        compiler_params=_cparams(("parallel", "arbitrary")),
        name=name,
    )(*args)


def _side_specs(w, steps, step_of, rows=None):
    cols = w.shape[1]
    rows = w.shape[0] if rows is None else rows
    slab = rows // steps
    assert slab * steps == rows and slab % BF16_SUBLANES == 0, (rows, steps)
    spec = pl.BlockSpec((slab, cols), lambda *g: (step_of(*g), 0))
    return spec, jax.ShapeDtypeStruct((rows, cols), BF16)


def _swiglu_up_kernel(h_ref, wg_ref, wu_ref, wd_ref, o_ref, wdb_ref):
    wdb_ref[...] = wd_ref[...].astype(BF16)
    h = h_ref[...]
    g = jnp.dot(h, wg_ref[...], preferred_element_type=F32)
    u = jnp.dot(h, wu_ref[...], preferred_element_type=F32)
    o_ref[...] = (g * jax.nn.sigmoid(g) * u).astype(o_ref.dtype)


def swiglu_up(h, wg, wu, wd, tm=2048, tn=256):
    m, k = h.shape
    n = wg.shape[1]
    tm = min(tm, m)
    tn = min(tn, n)
    nnb = n // tn
    side_spec, side_shape = _side_specs(wd, (m // tm) * nnb, lambda i, j: i * nnb + j)
    return pl.pallas_call(
        _swiglu_up_kernel,
        grid=(m // tm, nnb),
        in_specs=[pl.BlockSpec((tm, k), lambda i, j: (i, 0)),
                  pl.BlockSpec((k, tn), lambda i, j: (0, j)),
                  pl.BlockSpec((k, tn), lambda i, j: (0, j)),
                  side_spec],
        out_specs=[pl.BlockSpec((tm, tn), lambda i, j: (i, j)), side_spec],
        out_shape=[jax.ShapeDtypeStruct((m, n), BF16), side_shape],
        compiler_params=_cparams(("parallel", "parallel")),
        name="swiglu_up",
    )(h, wg, wu, wd)


def _matmul_k_kernel(*refs, has_res):
    a_ref, b_ref = refs[0], refs[1]
    o_ref, acc_ref = refs[-2], refs[-1]
    kk = pl.program_id(2)
    part = jnp.dot(a_ref[...], b_ref[...], preferred_element_type=F32)

    @pl.when(kk == 0)
    def _():
        acc_ref[...] = part + refs[2][...] if has_res else part

    @pl.when(kk > 0)
    def _():
        acc_ref[...] += part

    @pl.when(kk == pl.num_programs(2) - 1)
    def _():
        o_ref[...] = acc_ref[...].astype(o_ref.dtype)


def matmul_k(a, b, res, out_dtype, tm=1024, tn=512, tk=5504, name="matmul_k"):
    m, k = a.shape
    n = b.shape[1]
    tm = min(tm, m)
    tn = min(tn, n)
    tk = min(tk, k)
    in_specs = [pl.BlockSpec((tm, tk), lambda i, j, kk: (i, kk)),
                pl.BlockSpec((tk, tn), lambda i, j, kk: (kk, j))]
    args = [a, b]
    if res is not None:
        in_specs.append(pl.BlockSpec((tm, tn), lambda i, j, kk: (i, j)))
        args.append(res)
    return pl.pallas_call(
        functools.partial(_matmul_k_kernel, has_res=res is not None),
        grid=(m // tm, n // tn, k // tk),
        in_specs=in_specs,
        out_specs=pl.BlockSpec((tm, tn), lambda i, j, kk: (i, j)),
        out_shape=jax.ShapeDtypeStruct((m, n), out_dtype),
        scratch_shapes=[pltpu.VMEM((tm, tn), F32)],
        compiler_params=_cparams(("parallel", "parallel", "arbitrary")),
        name=name,
    )(*args)


def _select_kernel(q_ref, kw_ref, kwq_ref, sa_ref, sb_ref, bias_ref, sa16_ref, sb16_ref,
                   s_ref, ke_ref, ko_ref, *, tq, tk, n_chunks, topk, tie_iters):
    sa16_ref[...] = sa_ref[...].astype(BF16)
    sb16_ref[...] = sb_ref[...].astype(BF16)
    i = pl.program_id(1)

    @pl.when(i == 0)
    def _():
        lane = lax.broadcasted_iota(jnp.int32, (tk, LANES), 1)
        for c in range(n_chunks):
            kt = kw_ref[0, 0, :, c * tk:(c + 1) * tk].T
            ke_ref[c * tk:(c + 1) * tk, :] = jnp.where(lane < IDX_DIM, kt, 0.0).astype(BF16)
            ko_ref[c * tk:(c + 1) * tk, :] = jnp.where(
                lane >= IDX_DIM, pltpu.roll(kt, IDX_DIM, axis=1), 0.0).astype(BF16)

    t0 = i * tq
    nj = (t0 + tq + tk - 1) // tk
    qpos = t0 + lax.broadcasted_iota(jnp.int32, (1, tq), 1)
    w = kwq_ref[0, 0, IDX_DIM:IDX_DIM + IDX_HEADS, :]

    def score_chunk(j, carry):
        rmin, rmax = carry
        r0 = pl.multiple_of(j * tk, tk)
        ke = ke_ref[pl.ds(r0, tk), :]
        ko = ko_ref[pl.ds(r0, tk), :]
        acc = jnp.zeros((tk, tq), F32)
        for p in range(IDX_HEADS // 2):
            qp = q_ref[0, p]
            le = jnp.dot(ke, qp, preferred_element_type=F32)
            lo = jnp.dot(ko, qp, preferred_element_type=F32)
            acc = (acc + w[2 * p:2 * p + 1, :] * jnp.maximum(le, 0.0)
                   + w[2 * p + 1:2 * p + 2, :] * jnp.maximum(lo, 0.0))
        kpos = j * tk + lax.broadcasted_iota(jnp.int32, (tk, 1), 0)
        causal = kpos <= qpos
        s_ref[j] = jnp.where(causal, acc, NEG)
        rmax = jnp.maximum(rmax, jnp.max(jnp.where(causal, acc, NEG), axis=0, keepdims=True))
        rmin = jnp.minimum(rmin, jnp.min(jnp.where(causal, acc, BIG), axis=0, keepdims=True))
        return rmin, rmax

    rmin, rmax = lax.fori_loop(
        0, nj, score_chunk,
        (jnp.full((1, tq), BIG, F32), jnp.full((1, tq), NEG, F32)))

    n_causal = (qpos + 1).astype(F32)
    keff = jnp.minimum(n_causal, float(topk))
    nacc = 64

    def count_ge(thr):
        def body(j, c):
            part = jnp.where(s_ref[j] >= thr, 1.0, 0.0)
            return c + jnp.sum(part.reshape(tk // nacc, nacc, tq), axis=0)
        c = lax.fori_loop(0, nj, body, jnp.zeros((nacc, tq), F32))
        return jnp.sum(c, axis=0, keepdims=True)

    def any_true(x):
        return jnp.max(jnp.where(x, 1.0, 0.0)) > 0.0

    hi0 = rmax
    c_max = count_ge(hi0)
    at_max = c_max >= keff
    lo0 = jnp.where(at_max, hi0, rmin)
    c_lo0 = jnp.where(at_max, c_max, n_causal)
    c_hi0 = jnp.where(at_max, 0.0, c_max)

    def cond(st):
        it, lo, hi, c_lo, _ = st
        mid = 0.5 * (lo + hi)
        open_ = jnp.logical_and(c_lo != keff, jnp.logical_and(mid > lo, mid < hi))
        return jnp.logical_and(it < BISECT_MAX_ITERS, any_true(open_))

    def halve(lo, hi, c_lo, c_hi):
        mid = 0.5 * (lo + hi)
        c = count_ge(mid)
        ge = c >= keff
        return (jnp.where(ge, mid, lo), jnp.where(ge, hi, mid),
                jnp.where(ge, c, c_lo), jnp.where(ge, c_hi, c))

    def bisect(st):
        it, lo, hi, c_lo, c_hi = st
        lo, hi, c_lo, c_hi = halve(*halve(lo, hi, c_lo, c_hi))
        return it + 2, lo, hi, c_lo, c_hi

    _, lo, _, c_lo, c_hi = lax.while_loop(cond, bisect, (jnp.int32(0), lo0, hi0, c_lo0, c_hi0))

    tied = c_lo != keff
    want = keff - c_hi

    def count_tied_upto(jpos):
        def body(j, c):
            kposf = (j * tk + lax.broadcasted_iota(jnp.int32, (tk, 1), 0)).astype(F32)
            hit = jnp.logical_and(s_ref[j] == lo, kposf <= jpos)
            return c + jnp.sum(jnp.where(hit, 1.0, 0.0).reshape(tk // nacc, nacc, tq), axis=0)
        c = lax.fori_loop(0, nj, body, jnp.zeros((nacc, tq), F32))
        return jnp.sum(c, axis=0, keepdims=True)

    def tie_cond(st):
        it, jl, jh = st
        return jnp.logical_and(it < tie_iters, any_true(jnp.logical_and(tied, jh - jl > 1.0)))

    def tie_step(st):
        it, jl, jh = st
        jm = jnp.floor(0.5 * (jl + jh))
        ok = count_tied_upto(jm) >= want
        return it + 1, jnp.where(ok, jl, jm), jnp.where(ok, jm, jh)

    _, _, jh = lax.while_loop(tie_cond, tie_step,
                              (jnp.int32(0), jnp.full((1, tq), -1.0, F32), qpos.astype(F32)))
    jmax = jnp.where(tied, jh, BIG)
    any_tied = any_true(tied)

    def write_sel(j, c):
        r0 = pl.multiple_of(j * tk, tk)
        bias_ref[0, pl.ds(r0, tk), :] = jnp.where(s_ref[j] >= lo, 0.0, NEG).astype(bias_ref.dtype)
        return c

    def write_sel_tied(j, c):
        r0 = pl.multiple_of(j * tk, tk)
        s = s_ref[j]
        kposf = (j * tk + lax.broadcasted_iota(jnp.int32, (tk, 1), 0)).astype(F32)
        keep = jnp.logical_or(s > lo, jnp.logical_and(s == lo, kposf <= jmax))
        bias_ref[0, pl.ds(r0, tk), :] = jnp.where(keep, 0.0, NEG).astype(bias_ref.dtype)
        return c

    def write_none(j, c):
        r0 = pl.multiple_of(j * tk, tk)
        bias_ref[0, pl.ds(r0, tk), :] = jnp.full((tk, tq), NEG, bias_ref.dtype)
        return c

    @pl.when(jnp.logical_not(any_tied))
    def _():
        lax.fori_loop(0, nj, write_sel, 0)

    @pl.when(any_tied)
    def _():
        lax.fori_loop(0, nj, write_sel_tied, 0)

    lax.fori_loop(nj, n_chunks, write_none, 0)


def select_mask(qt, kw, side_a, side_b, *, tq, tk, topk):
    bsz, npair, _, seq = qt.shape
    n_chunks = seq // tk
    nqb = seq // tq
    step_of = lambda b, i: b * nqb + i
    spec_a, shape_a = _side_specs(side_a, bsz * nqb, step_of)
    spec_b, shape_b = _side_specs(side_b, bsz * nqb, step_of)
    return pl.pallas_call(
        functools.partial(_select_kernel, tq=tq, tk=tk, n_chunks=n_chunks, topk=topk,
                          tie_iters=seq.bit_length() + 1),
        grid=(bsz, nqb),
        in_specs=[pl.BlockSpec((1, npair, LANES, tq), lambda b, i: (b, 0, 0, i)),
                  pl.BlockSpec((1, 1, LANES, seq), lambda b, i: (b, 0, 0, 0)),
                  pl.BlockSpec((1, 1, LANES, tq), lambda b, i: (b, 0, 0, i)),
                  spec_a, spec_b],
        out_specs=[pl.BlockSpec((1, seq, tq), lambda b, i: (b, 0, i)), spec_a, spec_b],
        out_shape=[jax.ShapeDtypeStruct((bsz, seq, seq), BF16), shape_a, shape_b],
        scratch_shapes=[pltpu.VMEM((n_chunks, tk, tq), F32),
                        pltpu.VMEM((seq, LANES), BF16),
                        pltpu.VMEM((seq, LANES), BF16)],
        compiler_params=_cparams(("parallel", "arbitrary")),
        name="select_mask",
    )(qt, kw, kw, side_a, side_b)


def _attn_kernel(q_ref, k_ref, v_ref, b_ref, o_ref, m_ref, acc_ref, *, tq, tk, nh, dh):
    i = pl.program_id(1)
    j = pl.program_id(2)

    @pl.when(j == 0)
    def _():
        m_ref[...] = jnp.full(m_ref.shape, -jnp.inf, F32)
        acc_ref[...] = jnp.zeros(acc_ref.shape, F32)

    @pl.when(j * tk <= i * tq + tq - 1)
    def _():
        bias = b_ref[0]
        for h in range(nh):
            s = jnp.dot(k_ref[:, h * dh:(h + 1) * dh], q_ref[0, h], preferred_element_type=F32)
            sb = s.astype(BF16) + bias
            part = jnp.max(sb.reshape(tk // BF16_SUBLANES, BF16_SUBLANES, tq), axis=0)
            m_blk = jnp.max(part.astype(F32), axis=0, keepdims=True)
            m_old = m_ref[h]
            m_new = jnp.maximum(m_old, m_blk)
            p = jnp.exp2(sb - m_new[:1].astype(BF16))
            alpha = jnp.exp2(m_old - m_new)
            pv = jnp.dot(v_ref[0, h], p, preferred_element_type=F32)
            acc_ref[h] = alpha[:1] * acc_ref[h] + pv
            m_ref[h] = m_new

    @pl.when(j == pl.num_programs(2) - 1)
    def _():
        for h in range(nh):
            acc = acc_ref[h]
            o = acc[:dh] / acc[dh:dh + 1]
            o_ref[:, h * dh:(h + 1) * dh] = o.T.astype(o_ref.dtype)


def masked_attention(qt, kx, vt, bias, *, bsz, seq, tq, tk):
    nh, dh = ATT_HEADS, HEAD_DIM
    d_att = nh * dh
    nqb, nkb = seq // tq, seq // tk
    vrows = dh + ONES_ROWS

    def jc(i, j):
        return jnp.minimum(j, (i * tq + tq - 1) // tk)

    return pl.pallas_call(
        functools.partial(_attn_kernel, tq=tq, tk=tk, nh=nh, dh=dh),
        grid=(bsz, nqb, nkb),
        in_specs=[pl.BlockSpec((1, nh, dh, tq), lambda b, i, j: (b, 0, 0, i)),
                  pl.BlockSpec((tk, d_att), lambda b, i, j: (b * nkb + jc(i, j), 0)),
                  pl.BlockSpec((1, nh, vrows, tk), lambda b, i, j: (b, 0, 0, jc(i, j))),
                  pl.BlockSpec((1, tk, tq), lambda b, i, j: (b, jc(i, j), i))],
        out_specs=pl.BlockSpec((tq, d_att), lambda b, i, j: (b * nqb + i, 0)),
        out_shape=jax.ShapeDtypeStruct((bsz * seq, d_att), BF16),
        scratch_shapes=[pltpu.VMEM((nh, F32_SUBLANES, tq), F32),
                        pltpu.VMEM((nh, vrows, tq), F32)],
        compiler_params=_cparams(("parallel", "parallel", "arbitrary")),
        name="masked_attention",
    )(qt, kx, vt, bias)


def _gelu_tanh(x):
    return 0.5 * x * (1.0 + jnp.tanh(0.7978845608028654 * (x + 0.044715 * (x * x * x))))


def _gmlp_kernel(h_ref, wu_ref, wv_ref, gain_ref, w_ref, bias_ref, side_ref, o_ref, side16_ref,
                 *, n_sub, n_grp):
    side16_ref[...] = side_ref[...].astype(BF16)
    nt = (((1,), (1,)), ((), ()))
    h = h_ref[...]
    u_all = lax.dot_general(h, wu_ref[...], nt, preferred_element_type=F32)
    v_all = lax.dot_general(h, wv_ref[...], nt, preferred_element_type=F32)
    tri = (lax.broadcasted_iota(jnp.int32, (CHUNK, CHUNK), 1)
           <= lax.broadcasted_iota(jnp.int32, (CHUNK, CHUNK), 0))
    for g in range(n_grp):
        gs = slice(g * GMLP_CH, (g + 1) * GMLP_CH)
        wg = jnp.where(tri, w_ref[g], 0.0).astype(BF16)
        gain = gain_ref[:, gs]
        bias = bias_ref[:, gs]
        for c in range(n_sub):
            rs = slice(c * CHUNK, (c + 1) * CHUNK)
            u = _gelu_tanh(u_all[rs, gs])
            v = _gelu_tanh(v_all[rs, gs])
            mu = jnp.mean(v, axis=-1, keepdims=True)
            vc = v - mu
            var = jnp.mean(vc * vc, axis=-1, keepdims=True)
            vn = (vc * lax.rsqrt(var + EPS) * gain).astype(BF16)
            z = jnp.dot(wg, vn, preferred_element_type=F32) + bias
            o_ref[rs, gs] = (u * z).astype(o_ref.dtype)


def gmlp_mixer(h, w_uv_t, v_gain, w_s, b_s, side, tm=1024, n_grp=4):
    m, k = h.shape
    d = GMLP_GROUPS * GMLP_CH
    tm = min(tm, m)
    tn = n_grp * GMLP_CH
    ngb = d // tn
    nmb = m // tm
    gain = v_gain.reshape(1, d).astype(F32)
    bias_full = jnp.repeat(b_s.T.astype(F32), GMLP_CH, axis=1)
    side_spec, side_shape = _side_specs(side, ngb * nmb, lambda j, i: j * nmb + i)
    return pl.pallas_call(
        functools.partial(_gmlp_kernel, n_sub=tm // CHUNK, n_grp=n_grp),
        grid=(ngb, nmb),
        in_specs=[pl.BlockSpec((tm, k), lambda j, i: (i, 0)),
                  pl.BlockSpec((tn, k), lambda j, i: (j, 0)),
                  pl.BlockSpec((tn, k), lambda j, i: (j + ngb, 0)),
                  pl.BlockSpec((1, tn), lambda j, i: (0, j)),
                  pl.BlockSpec((n_grp, CHUNK, CHUNK), lambda j, i: (j, 0, 0)),
                  pl.BlockSpec((CHUNK, tn), lambda j, i: (0, j)),
                  side_spec],
        out_specs=[pl.BlockSpec((tm, tn), lambda j, i: (i, j)), side_spec],
        out_shape=[jax.ShapeDtypeStruct((m, d), BF16), side_shape],
        compiler_params=_cparams(("parallel", "parallel")),
        name="gmlp_mixer",
    )(h, w_uv_t, w_uv_t, gain, w_s.astype(F32), bias_full, side)


def dsa_attention(qt, kx, vt, qit, kw, side_a, side_b, *, bsz, seq,
                  tq_sel=256, tq_att=1024, tk_sel=512, tk_att=1024):
    tq_sel, tk_sel = min(tq_sel, seq), min(tk_sel, seq)
    tq_att, tk_att = min(tq_att, seq), min(tk_att, seq)
    topk = min(TOPK_MAX, seq // 4)
    bias, side_a16, side_b16 = select_mask(qit, kw, side_a, side_b,
                                           tq=tq_sel, tk=tk_sel, topk=topk)
    att = masked_attention(qt, kx, vt, bias, bsz=bsz, seq=seq, tq=tq_att, tk=tk_att)
    return att, side_a16, side_b16


def _in_proj_scale(d_att, n_idx, d_gm):
    return jnp.concatenate([
        jnp.full((d_att,), LOG2E * HEAD_DIM ** -0.5, F32),
        jnp.ones((2 * d_att,), F32),
        jnp.full((n_idx,), IDX_DIM ** -0.5, F32),
        jnp.ones((IDX_DIM,), F32),
        jnp.full((IDX_HEADS,), IDX_HEADS ** -0.5, F32),
        jnp.ones((2 * d_gm,), F32)]).reshape(-1, 1)


def kernel(x, norm_mix, w_in, gmlp_v_gain, w_spatial, b_spatial, w_out, norm_ffn,
           w_gate, w_up, w_down, norm_final):
    bsz, seq, d_model = x.shape
    m = bsz * seq
    d_att = ATT_HEADS * HEAD_DIM
    d_gm = GMLP_GROUPS * GMLP_CH
    n_idx = IDX_HEADS * IDX_DIM
    depth = w_in.shape[0]
    tn = 512
    scale = _in_proj_scale(d_att, n_idx, d_gm)
    xf = x.reshape(m, d_model)
    for i in range(depth):
        wt = jnp.swapaxes(w_in[i], 0, 1)
        uv_off = 3 * d_att + n_idx + IDX_DIM + IDX_HEADS
        w_uv = wt[uv_off:].astype(BF16)

        h = rmsnorm(xf, norm_mix[i], BF16)
        n_qkvi = 3 * d_att + n_idx
        qt, w16 = matmul([h], [(wt, 0)], BF16, n=d_att, col_off=0, w_t=True, scale=scale,
                         side=wt, side_rows=n_qkvi, out_mode="slabs_t", seq=seq, tn=tn,
                         name="proj_q")
        tnb = 1024
        kx = matmul([h], [(w16, 0)], BF16, n=d_att, col_off=d_att // tnb, w_t=True,
                    tn=tnb, name="proj_k")
        vt = matmul([h], [(w16, 0)], BF16, n=d_att, col_off=2 * d_att // tnb, w_t=True,
                    out_mode="slabs_t_ones", seq=seq, tn=tnb, name="proj_v")
        qit = matmul([h], [(w16, 0)], BF16, n=n_idx, col_off=3 * d_att // tnb, w_t=True,
                     post_scale=IDX_DIM ** -0.5, out_mode="slabs_t", seq=seq, tn=tnb,
                     name="proj_qidx")
        kw = matmul([h], [(wt, 0)], F32, n=LANES, col_off=(3 * d_att + n_idx) // LANES,
                    w_t=True, scale=scale, out_mode="slabs_t", seq=seq, tn=LANES, name="proj_small")
        gm, wo = gmlp_mixer(h, w_uv, gmlp_v_gain[i], w_spatial[i], b_spatial[i], w_out[i])

        att, wg, wu = dsa_attention(qt, kx, vt, qit, kw, w_gate[i], w_up[i], bsz=bsz, seq=seq)

        xf = matmul([att, gm], [(wo, 0), (wo, 1)], F32, n=d_model, res=xf, tn=1024, name="out_proj")

        h2 = rmsnorm(xf, norm_ffn[i], BF16)
        hid, wd = swiglu_up(h2, wg, wu, w_down[i])
        if i + 1 < depth:
            xf = matmul_k(hid, wd, xf, F32, name="ffn_down")
    y = matmul_k(hid, wd, None, BF16, tn=1024, name="ffn_down")
    out = rmsnorm(xf, norm_final, x.dtype, add=y)
    return out.reshape(bsz, seq, d_model)
```

```python
import functools

import jax
import jax.numpy as jnp
from jax import lax
from jax.experimental import pallas as pl
from jax.experimental.pallas import tpu as pltpu

F32 = jnp.float32
BF16 = jnp.bfloat16

ATT_HEADS = 16
HEAD_DIM = 128
IDX_HEADS = 32
IDX_DIM = 64
TOPK_MAX = 256
GMLP_GROUPS = 16
GMLP_CH = 128
CHUNK = 128
EPS = 1e-6
NEG = -1e30
BIG = 1e30
LOG2E = 1.4426950408889634

LANES = 128
F32_SUBLANES = 8
BF16_SUBLANES = 16
ONES_ROWS = BF16_SUBLANES
VMEM_LIMIT = 56 * 1024 * 1024
BISECT_MAX_ITERS = 320


def _cparams(sem):
    return pltpu.CompilerParams(dimension_semantics=sem, vmem_limit_bytes=VMEM_LIMIT)


def _rmsnorm_kernel(*refs, has_add):
    x_ref, g_ref, o_ref = refs[0], refs[-2], refs[-1]
    x = x_ref[...].astype(F32)
    if has_add:
        x = x + refs[1][...].astype(F32)
    ms = jnp.mean(x * x, axis=-1, keepdims=True)
    o_ref[...] = (x * lax.rsqrt(ms + EPS) * g_ref[...]).astype(o_ref.dtype)


def rmsnorm(x, g, out_dtype, add=None, tm=512):
    m, d = x.shape
    tm = min(tm, m)
    row_spec = pl.BlockSpec((tm, d), lambda i: (i, 0))
    ins = [x] if add is None else [x, add]
    return pl.pallas_call(
        functools.partial(_rmsnorm_kernel, has_add=add is not None),
        grid=(m // tm,),
        in_specs=[row_spec] * len(ins) + [pl.BlockSpec((1, d), lambda i: (0, 0))],
        out_specs=row_spec,
        out_shape=jax.ShapeDtypeStruct((m, d), out_dtype),
        compiler_params=_cparams(("parallel",)),
        name="rmsnorm",
    )(*ins, g.reshape(1, d).astype(F32))


def _matmul_kernel(*refs, n_pairs, has_scale, has_res, has_side, cast_w, w_t, out_mode,
                   post_scale):
    a_refs = refs[:n_pairs]
    w_refs = refs[n_pairs:2 * n_pairs]
    pos = 2 * n_pairs
    sc_ref = refs[pos] if has_scale else None
    pos += int(has_scale)
    r_ref = refs[pos] if has_res else None
    pos += int(has_res)
    side_ref = refs[pos] if has_side else None
    pos += int(has_side)
    o_ref = refs[pos]
    pos += 1
    if has_side:
        refs[pos][...] = side_ref[...].astype(BF16)
        pos += 1
    wb_refs = refs[pos:]

    if cast_w:
        @pl.when(pl.program_id(1) == 0)
        def _():
            for p in range(n_pairs):
                w = w_refs[p][...]
                if has_scale:
                    w = w * sc_ref[...]
                wb_refs[p][...] = w.astype(BF16)
        rhs = wb_refs
    else:
        rhs = w_refs

    nt = (((1,), (1,)), ((), ()))
    if out_mode in ("slabs_t", "slabs_t_ones"):
        acc = lax.dot_general(rhs[0][...], a_refs[0][...], nt, preferred_element_type=F32)
        if post_scale is not None:
            acc = acc * post_scale
        for pp in range(acc.shape[0] // LANES):
            o_ref[0, pp, :LANES, :] = acc[pp * LANES:(pp + 1) * LANES, :].astype(o_ref.dtype)
            if out_mode == "slabs_t_ones":
                o_ref[0, pp, LANES:, :] = jnp.ones((ONES_ROWS, acc.shape[1]), o_ref.dtype)
        return

    def mm(p):
        if w_t:
            return lax.dot_general(a_refs[p][...], rhs[p][...], nt, preferred_element_type=F32)
        return jnp.dot(a_refs[p][...], rhs[p][...], preferred_element_type=F32)

    acc = mm(0)
    for p in range(1, n_pairs):
        acc = acc + mm(p)
    if post_scale is not None:
        acc = acc * post_scale
    if has_res:
        acc = acc + r_ref[...]

    o_ref[...] = acc.astype(o_ref.dtype)


def matmul(a_list, w_list, out_dtype, *, n, col_off=0, w_t=False, scale=None, post_scale=None,
           res=None, side=None, side_rows=None, out_mode="plain", seq=None, tm=1024, tn=512,
           name="matmul"):
    n_pairs = len(a_list)
    m = a_list[0].shape[0]
    tm = min(tm, m)
    tn = min(tn, n)
    cast_w = w_list[0][0].dtype != BF16
    in_specs, args, scratch = [], [], []
    for a in a_list:
        in_specs.append(pl.BlockSpec((tm, a.shape[1]), lambda j, i: (i, 0)))
        args.append(a)
    for a, (w, rb) in zip(a_list, w_list):
        k = a.shape[1]
        if w_t:
            in_specs.append(pl.BlockSpec((tn, k), lambda j, i: (j + col_off, 0)))
        else:
            in_specs.append(pl.BlockSpec((k, tn), lambda j, i, rb=rb: (rb, j + col_off)))
        args.append(w)
        if cast_w:
            scratch.append(pltpu.VMEM((tn, k) if w_t else (k, tn), BF16))
    if scale is not None:
        assert w_t and cast_w
        in_specs.append(pl.BlockSpec((tn, 1), lambda j, i: (j + col_off, 0)))
        args.append(scale)
    if res is not None:
        in_specs.append(pl.BlockSpec((tm, tn), lambda j, i: (i, j)))
        args.append(res)
    nmb_all = m // tm
    if side is not None:
        side_spec, side_shape = _side_specs(side, (n // tn) * nmb_all,
                                            lambda j, i: j * nmb_all + i, rows=side_rows)
        in_specs.append(side_spec)
        args.append(side)
    if out_mode == "plain":
        out_spec = pl.BlockSpec((tm, tn), lambda j, i: (i, j))
        out_shape = jax.ShapeDtypeStruct((m, n), out_dtype)
    else:
        assert w_t and n_pairs == 1 and res is None
        nmb = seq // tm
        rows = LANES + (ONES_ROWS if out_mode == "slabs_t_ones" else 0)
        out_spec = pl.BlockSpec((1, tn // LANES, rows, tm), lambda j, i: (i // nmb, j, 0, i % nmb))
        out_shape = jax.ShapeDtypeStruct((m // seq, n // LANES, rows, seq), out_dtype)
    if side is not None:
        out_spec, out_shape = [out_spec, side_spec], [out_shape, side_shape]
    return pl.pallas_call(
        functools.partial(_matmul_kernel, n_pairs=n_pairs, has_scale=scale is not None,
                          has_res=res is not None, has_side=side is not None,
                          cast_w=cast_w, w_t=w_t, out_mode=out_mode, post_scale=post_scale),
        grid=(n // tn, m // tm),
        in_specs=in_specs,
        out_specs=out_spec,
        out_shape=out_shape,
        scratch_shapes=scratch,
        compiler_params=_cparams(("parallel", "arbitrary")),
        name=name,
    )(*args)


def _side_specs(w, steps, step_of, rows=None):
    cols = w.shape[1]
    rows = w.shape[0] if rows is None else rows
    slab = rows // steps
    assert slab * steps == rows and slab % BF16_SUBLANES == 0, (rows, steps)
    spec = pl.BlockSpec((slab, cols), lambda *g: (step_of(*g), 0))
    return spec, jax.ShapeDtypeStruct((rows, cols), BF16)


def _swiglu_up_kernel(h_ref, wg_ref, wu_ref, wd_ref, o_ref, wdb_ref):
    wdb_ref[...] = wd_ref[...].astype(BF16)
    h = h_ref[...]
    g = jnp.dot(h, wg_ref[...], preferred_element_type=F32)
    u = jnp.dot(h, wu_ref[...], preferred_element_type=F32)
    o_ref[...] = (g * jax.nn.sigmoid(g) * u).astype(o_ref.dtype)


def swiglu_up(h, wg, wu, wd, tm=2048, tn=256):
    m, k = h.shape
    n = wg.shape[1]
    tm = min(tm, m)
    tn = min(tn, n)
    nnb = n // tn
    side_spec, side_shape = _side_specs(wd, (m // tm) * nnb, lambda i, j: i * nnb + j)
    return pl.pallas_call(
        _swiglu_up_kernel,
        grid=(m // tm, nnb),
        in_specs=[pl.BlockSpec((tm, k), lambda i, j: (i, 0)),
                  pl.BlockSpec((k, tn), lambda i, j: (0, j)),
                  pl.BlockSpec((k, tn), lambda i, j: (0, j)),
                  side_spec],
        out_specs=[pl.BlockSpec((tm, tn), lambda i, j: (i, j)), side_spec],
        out_shape=[jax.ShapeDtypeStruct((m, n), BF16), side_shape],
        compiler_params=_cparams(("parallel", "parallel")),
        name="swiglu_up",
    )(h, wg, wu, wd)


def _matmul_k_kernel(*refs, has_res, nk):
    a_ref, b_ref = refs[0], refs[1]
    o_ref, acc_ref = refs[-2], refs[-1]
    kk = pl.program_id(2)
    part = jnp.dot(a_ref[...], b_ref[...], preferred_element_type=F32)
    if has_res:
        first = part + refs[2][...]
    else:
        first = part
    if nk == 1:
        o_ref[...] = first.astype(o_ref.dtype)
        return

    @pl.when(kk == 0)
    def _():
        acc_ref[...] = first

    @pl.when(jnp.logical_and(kk > 0, kk < nk - 1))
    def _():
        acc_ref[...] += part

    @pl.when(kk == nk - 1)
    def _():
        o_ref[...] = (acc_ref[...] + part).astype(o_ref.dtype)


def matmul_k(a, b, res, out_dtype, tm=1024, tn=512, tk=5504, name="matmul_k"):
    m, k = a.shape
    n = b.shape[1]
    tm = min(tm, m)
    tn = min(tn, n)
    tk = min(tk, k)
    in_specs = [pl.BlockSpec((tm, tk), lambda i, j, kk: (i, kk)),
                pl.BlockSpec((tk, tn), lambda i, j, kk: (kk, j))]
    args = [a, b]
    if res is not None:
        in_specs.append(pl.BlockSpec((tm, tn), lambda i, j, kk: (i, j)))
        args.append(res)
    return pl.pallas_call(
        functools.partial(_matmul_k_kernel, has_res=res is not None, nk=k // tk),
        grid=(m // tm, n // tn, k // tk),
        in_specs=in_specs,
        out_specs=pl.BlockSpec((tm, tn), lambda i, j, kk: (i, j)),
        out_shape=jax.ShapeDtypeStruct((m, n), out_dtype),
        scratch_shapes=[pltpu.VMEM((tm, tn), F32)],
        compiler_params=_cparams(("parallel", "parallel", "arbitrary")),
        name=name,
    )(*args)


def _select_kernel(q_ref, kw_ref, kwq_ref, sa_ref, sb_ref, bias_ref, sa16_ref, sb16_ref,
                   s_ref, ke_ref, ko_ref, *, tq, tk, n_chunks, topk, tie_iters):
    sa16_ref[...] = sa_ref[...].astype(BF16)
    sb16_ref[...] = sb_ref[...].astype(BF16)
    i = pl.program_id(1)

    @pl.when(i == 0)
    def _():
        lane = lax.broadcasted_iota(jnp.int32, (tk, LANES), 1)
        for c in range(n_chunks):
            kt = kw_ref[0, 0, :, c * tk:(c + 1) * tk].T
            ke_ref[c * tk:(c + 1) * tk, :] = jnp.where(lane < IDX_DIM, kt, 0.0).astype(BF16)
            ko_ref[c * tk:(c + 1) * tk, :] = jnp.where(
                lane >= IDX_DIM, pltpu.roll(kt, IDX_DIM, axis=1), 0.0).astype(BF16)

    t0 = i * tq
    nj = (t0 + tq + tk - 1) // tk
    qpos = t0 + lax.broadcasted_iota(jnp.int32, (1, tq), 1)
    w = kwq_ref[0, 0, IDX_DIM:IDX_DIM + IDX_HEADS, :]

    def score_chunk(j, carry):
        rmin, rmax = carry
        r0 = pl.multiple_of(j * tk, tk)
        ke = ke_ref[pl.ds(r0, tk), :]
        ko = ko_ref[pl.ds(r0, tk), :]
        acc = jnp.zeros((tk, tq), F32)
        for p in range(IDX_HEADS // 2):
            qp = q_ref[0, p]
            le = jnp.dot(ke, qp, preferred_element_type=F32)
            lo = jnp.dot(ko, qp, preferred_element_type=F32)
            acc = (acc + w[2 * p:2 * p + 1, :] * jnp.maximum(le, 0.0)
                   + w[2 * p + 1:2 * p + 2, :] * jnp.maximum(lo, 0.0))
        kpos = j * tk + lax.broadcasted_iota(jnp.int32, (tk, 1), 0)
        causal = kpos <= qpos
        s_ref[j] = jnp.where(causal, acc, NEG)
        rmax = jnp.maximum(rmax, jnp.max(jnp.where(causal, acc, NEG), axis=0, keepdims=True))
        rmin = jnp.minimum(rmin, jnp.min(jnp.where(causal, acc, BIG), axis=0, keepdims=True))
        return rmin, rmax

    rmin, rmax = lax.fori_loop(
        0, nj, score_chunk,
        (jnp.full((1, tq), BIG, F32), jnp.full((1, tq), NEG, F32)))

    n_causal = (qpos + 1).astype(F32)
    keff = jnp.minimum(n_causal, float(topk))
    nacc = 64

    def count_ge(thr):
        def body(j, c):
            part = jnp.where(s_ref[j] >= thr, 1.0, 0.0)
            return c + jnp.sum(part.reshape(tk // nacc, nacc, tq), axis=0)
        c = lax.fori_loop(0, nj, body, jnp.zeros((nacc, tq), F32))
        return jnp.sum(c, axis=0, keepdims=True)

    def any_true(x):
        return jnp.max(jnp.where(x, 1.0, 0.0)) > 0.0

    hi0 = rmax
    c_max = count_ge(hi0)
    at_max = c_max >= keff
    lo0 = jnp.where(at_max, hi0, rmin)
    c_lo0 = jnp.where(at_max, c_max, n_causal)
    c_hi0 = jnp.where(at_max, 0.0, c_max)

    def cond(st):
        it, lo, hi, c_lo, _ = st
        mid = 0.5 * (lo + hi)
        open_ = jnp.logical_and(c_lo != keff, jnp.logical_and(mid > lo, mid < hi))
        return jnp.logical_and(it < BISECT_MAX_ITERS, any_true(open_))

    def halve(lo, hi, c_lo, c_hi):
        mid = 0.5 * (lo + hi)
        c = count_ge(mid)
        ge = c >= keff
        return (jnp.where(ge, mid, lo), jnp.where(ge, hi, mid),
                jnp.where(ge, c, c_lo), jnp.where(ge, c_hi, c))

    def bisect(st):
        it, lo, hi, c_lo, c_hi = st
        lo, hi, c_lo, c_hi = halve(*halve(lo, hi, c_lo, c_hi))
        return it + 2, lo, hi, c_lo, c_hi

    _, lo, _, c_lo, c_hi = lax.while_loop(cond, bisect, (jnp.int32(0), lo0, hi0, c_lo0, c_hi0))

    tied = c_lo != keff
    want = keff - c_hi

    def count_tied_upto(jpos):
        def body(j, c):
            kposf = (j * tk + lax.broadcasted_iota(jnp.int32, (tk, 1), 0)).astype(F32)
            hit = jnp.logical_and(s_ref[j] == lo, kposf <= jpos)
            return c + jnp.sum(jnp.where(hit, 1.0, 0.0).reshape(tk // nacc, nacc, tq), axis=0)
        c = lax.fori_loop(0, nj, body, jnp.zeros((nacc, tq), F32))
        return jnp.sum(c, axis=0, keepdims=True)

    def tie_cond(st):
        it, jl, jh = st
        return jnp.logical_and(it < tie_iters, any_true(jnp.logical_and(tied, jh - jl > 1.0)))

    def tie_step(st):
        it, jl, jh = st
        jm = jnp.floor(0.5 * (jl + jh))
        ok = count_tied_upto(jm) >= want
        return it + 1, jnp.where(ok, jl, jm), jnp.where(ok, jm, jh)

    _, _, jh = lax.while_loop(tie_cond, tie_step,
                              (jnp.int32(0), jnp.full((1, tq), -1.0, F32), qpos.astype(F32)))
    jmax = jnp.where(tied, jh, BIG)
    any_tied = any_true(tied)

    def write_sel(j, c):
        r0 = pl.multiple_of(j * tk, tk)
        bias_ref[0, pl.ds(r0, tk), :] = jnp.where(s_ref[j] >= lo, 0.0, NEG).astype(bias_ref.dtype)
        return c

    def write_sel_tied(j, c):
        r0 = pl.multiple_of(j * tk, tk)
        s = s_ref[j]
        kposf = (j * tk + lax.broadcasted_iota(jnp.int32, (tk, 1), 0)).astype(F32)
        keep = jnp.logical_or(s > lo, jnp.logical_and(s == lo, kposf <= jmax))
        bias_ref[0, pl.ds(r0, tk), :] = jnp.where(keep, 0.0, NEG).astype(bias_ref.dtype)
        return c

    def write_none(j, c):
        r0 = pl.multiple_of(j * tk, tk)
        bias_ref[0, pl.ds(r0, tk), :] = jnp.full((tk, tq), NEG, bias_ref.dtype)
        return c

    @pl.when(jnp.logical_not(any_tied))
    def _():
        lax.fori_loop(0, nj, write_sel, 0)

    @pl.when(any_tied)
    def _():
        lax.fori_loop(0, nj, write_sel_tied, 0)

    lax.fori_loop(nj, n_chunks, write_none, 0)


def select_mask(qt, kw, side_a, side_b, *, tq, tk, topk):
    bsz, npair, _, seq = qt.shape
    n_chunks = seq // tk
    nqb = seq // tq
    step_of = lambda b, i: b * nqb + i
    spec_a, shape_a = _side_specs(side_a, bsz * nqb, step_of)
    spec_b, shape_b = _side_specs(side_b, bsz * nqb, step_of)
    return pl.pallas_call(
        functools.partial(_select_kernel, tq=tq, tk=tk, n_chunks=n_chunks, topk=topk,
                          tie_iters=seq.bit_length() + 1),
        grid=(bsz, nqb),
        in_specs=[pl.BlockSpec((1, npair, LANES, tq), lambda b, i: (b, 0, 0, i)),
                  pl.BlockSpec((1, 1, LANES, seq), lambda b, i: (b, 0, 0, 0)),
                  pl.BlockSpec((1, 1, LANES, tq), lambda b, i: (b, 0, 0, i)),
                  spec_a, spec_b],
        out_specs=[pl.BlockSpec((1, seq, tq), lambda b, i: (b, 0, i)), spec_a, spec_b],
        out_shape=[jax.ShapeDtypeStruct((bsz, seq, seq), BF16), shape_a, shape_b],
        scratch_shapes=[pltpu.VMEM((n_chunks, tk, tq), F32),
                        pltpu.VMEM((seq, LANES), BF16),
                        pltpu.VMEM((seq, LANES), BF16)],
        compiler_params=_cparams(("parallel", "arbitrary")),
        name="select_mask",
    )(qt, kw, kw, side_a, side_b)


def _attn_kernel(q_ref, k_ref, v_ref, b_ref, o_ref, m_ref, acc_ref, *, tq, tk, nh, dh):
    i = pl.program_id(1)
    j = pl.program_id(2)

    @pl.when(j == 0)
    def _():
        m_ref[...] = jnp.full(m_ref.shape, -jnp.inf, F32)
        acc_ref[...] = jnp.zeros(acc_ref.shape, F32)

    @pl.when(j * tk <= i * tq + tq - 1)
    def _():
        bias = b_ref[0]
        for h in range(nh):
            s = jnp.dot(k_ref[:, h * dh:(h + 1) * dh], q_ref[0, h], preferred_element_type=F32)
            sb = s.astype(BF16) + bias
            part = jnp.max(sb.reshape(tk // BF16_SUBLANES, BF16_SUBLANES, tq), axis=0)
            m_blk = jnp.max(part.astype(F32), axis=0, keepdims=True)
            m_old = m_ref[h]
            m_new = jnp.maximum(m_old, m_blk)
            p = jnp.exp2(sb - m_new[:1].astype(BF16))
            alpha = jnp.exp2(m_old - m_new)
            pv = jnp.dot(v_ref[0, h], p, preferred_element_type=F32)
            acc_ref[h] = alpha[:1] * acc_ref[h] + pv
            m_ref[h] = m_new

    @pl.when(j == pl.num_programs(2) - 1)
    def _():
        for h in range(nh):
            acc = acc_ref[h]
            o = acc[:dh] / acc[dh:dh + 1]
            o_ref[:, h * dh:(h + 1) * dh] = o.T.astype(o_ref.dtype)


def masked_attention(qt, kx, vt, bias, *, bsz, seq, tq, tk):
    nh, dh = ATT_HEADS, HEAD_DIM
    d_att = nh * dh
    nqb, nkb = seq // tq, seq // tk
    vrows = dh + ONES_ROWS

    def jc(i, j):
        return jnp.minimum(j, (i * tq + tq - 1) // tk)

    return pl.pallas_call(
        functools.partial(_attn_kernel, tq=tq, tk=tk, nh=nh, dh=dh),
        grid=(bsz, nqb, nkb),
        in_specs=[pl.BlockSpec((1, nh, dh, tq), lambda b, i, j: (b, 0, 0, i)),
                  pl.BlockSpec((tk, d_att), lambda b, i, j: (b * nkb + jc(i, j), 0)),
                  pl.BlockSpec((1, nh, vrows, tk), lambda b, i, j: (b, 0, 0, jc(i, j))),
                  pl.BlockSpec((1, tk, tq), lambda b, i, j: (b, jc(i, j), i))],
        out_specs=pl.BlockSpec((tq, d_att), lambda b, i, j: (b * nqb + i, 0)),
        out_shape=jax.ShapeDtypeStruct((bsz * seq, d_att), BF16),
        scratch_shapes=[pltpu.VMEM((nh, F32_SUBLANES, tq), F32),
                        pltpu.VMEM((nh, vrows, tq), F32)],
        compiler_params=_cparams(("parallel", "parallel", "arbitrary")),
        name="masked_attention",
    )(qt, kx, vt, bias)


def _gelu_tanh(x):
    return 0.5 * x * (1.0 + jnp.tanh(0.7978845608028654 * (x + 0.044715 * (x * x * x))))


def _gmlp_kernel(h_ref, wu_ref, wv_ref, gain_ref, w_ref, bias_ref, side_ref, o_ref, side16_ref,
                 *, n_sub, n_grp):
    side16_ref[...] = side_ref[...].astype(BF16)
    nt = (((1,), (1,)), ((), ()))
    h = h_ref[...]
    u_all = lax.dot_general(h, wu_ref[...], nt, preferred_element_type=F32)
    v_all = lax.dot_general(h, wv_ref[...], nt, preferred_element_type=F32)
    tri = (lax.broadcasted_iota(jnp.int32, (CHUNK, CHUNK), 1)
           <= lax.broadcasted_iota(jnp.int32, (CHUNK, CHUNK), 0))
    for g in range(n_grp):
        gs = slice(g * GMLP_CH, (g + 1) * GMLP_CH)
        wg = jnp.where(tri, w_ref[g], 0.0).astype(BF16)
        gain = gain_ref[:, gs]
        bias = bias_ref[:, gs]
        for c in range(n_sub):
            rs = slice(c * CHUNK, (c + 1) * CHUNK)
            u = _gelu_tanh(u_all[rs, gs])
            v = _gelu_tanh(v_all[rs, gs])
            mu = jnp.mean(v, axis=-1, keepdims=True)
            vc = v - mu
            var = jnp.mean(vc * vc, axis=-1, keepdims=True)
            vn = (vc * lax.rsqrt(var + EPS) * gain).astype(BF16)
            z = jnp.dot(wg, vn, preferred_element_type=F32) + bias
            o_ref[rs, gs] = (u * z).astype(o_ref.dtype)


def gmlp_mixer(h, w_uv_t, v_gain, w_s, b_s, side, tm=1024, n_grp=4):
    m, k = h.shape
    d = GMLP_GROUPS * GMLP_CH
    tm = min(tm, m)
    tn = n_grp * GMLP_CH
    ngb = d // tn
    nmb = m // tm
    gain = v_gain.reshape(1, d).astype(F32)
    bias_full = jnp.repeat(b_s.T.astype(F32), GMLP_CH, axis=1)
    side_spec, side_shape = _side_specs(side, ngb * nmb, lambda j, i: j * nmb + i)
    return pl.pallas_call(
        functools.partial(_gmlp_kernel, n_sub=tm // CHUNK, n_grp=n_grp),
        grid=(ngb, nmb),
        in_specs=[pl.BlockSpec((tm, k), lambda j, i: (i, 0)),
                  pl.BlockSpec((tn, k), lambda j, i: (j, 0)),
                  pl.BlockSpec((tn, k), lambda j, i: (j + ngb, 0)),
                  pl.BlockSpec((1, tn), lambda j, i: (0, j)),
                  pl.BlockSpec((n_grp, CHUNK, CHUNK), lambda j, i: (j, 0, 0)),
                  pl.BlockSpec((CHUNK, tn), lambda j, i: (0, j)),
                  side_spec],
        out_specs=[pl.BlockSpec((tm, tn), lambda j, i: (i, j)), side_spec],
        out_shape=[jax.ShapeDtypeStruct((m, d), BF16), side_shape],
        compiler_params=_cparams(("parallel", "parallel")),
        name="gmlp_mixer",
    )(h, w_uv_t, w_uv_t, gain, w_s.astype(F32), bias_full, side)


def dsa_attention(qt, kx, vt, qit, kw, side_a, side_b, *, bsz, seq,
                  tq_sel=256, tq_att=1024, tk_sel=512, tk_att=1024):
    tq_sel, tk_sel = min(tq_sel, seq), min(tk_sel, seq)
    tq_att, tk_att = min(tq_att, seq), min(tk_att, seq)
    topk = min(TOPK_MAX, seq // 4)
    bias, side_a16, side_b16 = select_mask(qit, kw, side_a, side_b,
                                           tq=tq_sel, tk=tk_sel, topk=topk)
    att = masked_attention(qt, kx, vt, bias, bsz=bsz, seq=seq, tq=tq_att, tk=tk_att)
    return att, side_a16, side_b16


def _in_proj_scale(d_att, n_idx, d_gm):
    return jnp.concatenate([
        jnp.full((d_att,), LOG2E * HEAD_DIM ** -0.5, F32),
        jnp.ones((2 * d_att,), F32),
        jnp.full((n_idx,), IDX_DIM ** -0.5, F32),
        jnp.ones((IDX_DIM,), F32),
        jnp.full((IDX_HEADS,), IDX_HEADS ** -0.5, F32),
        jnp.ones((2 * d_gm,), F32)]).reshape(-1, 1)


def kernel(x, norm_mix, w_in, gmlp_v_gain, w_spatial, b_spatial, w_out, norm_ffn,
           w_gate, w_up, w_down, norm_final):
    bsz, seq, d_model = x.shape
    m = bsz * seq
    d_att = ATT_HEADS * HEAD_DIM
    d_gm = GMLP_GROUPS * GMLP_CH
    n_idx = IDX_HEADS * IDX_DIM
    depth = w_in.shape[0]
    tn = 512
    scale = _in_proj_scale(d_att, n_idx, d_gm)
    xf = x.reshape(m, d_model)
    for i in range(depth):
        wt = jnp.swapaxes(w_in[i], 0, 1)
        uv_off = 3 * d_att + n_idx + IDX_DIM + IDX_HEADS
        w_uv = wt[uv_off:].astype(BF16)

        h = rmsnorm(xf, norm_mix[i], BF16)
        n_qkvi = 3 * d_att + n_idx
        qt, w16 = matmul([h], [(wt, 0)], BF16, n=d_att, col_off=0, w_t=True, scale=scale,
                         side=wt, side_rows=n_qkvi, out_mode="slabs_t", seq=seq, tn=tn,
                         name="proj_q")
        tnb = 1024
        kx = matmul([h], [(w16, 0)], BF16, n=d_att, col_off=d_att // tnb, w_t=True,
                    tn=tnb, name="proj_k")
        vt = matmul([h], [(w16, 0)], BF16, n=d_att, col_off=2 * d_att // tnb, w_t=True,
                    out_mode="slabs_t_ones", seq=seq, tn=tnb, name="proj_v")
        qit = matmul([h], [(w16, 0)], BF16, n=n_idx, col_off=3 * d_att // tnb, w_t=True,
                     post_scale=IDX_DIM ** -0.5, out_mode="slabs_t", seq=seq, tn=tnb,
                     name="proj_qidx")
        kw = matmul([h], [(wt, 0)], F32, n=LANES, col_off=(3 * d_att + n_idx) // LANES,
                    w_t=True, scale=scale, out_mode="slabs_t", seq=seq, tn=LANES, name="proj_small")
        gm, wo = gmlp_mixer(h, w_uv, gmlp_v_gain[i], w_spatial[i], b_spatial[i], w_out[i])

        att, wg, wu = dsa_attention(qt, kx, vt, qit, kw, w_gate[i], w_up[i], bsz=bsz, seq=seq)

        xf = matmul([att, gm], [(wo, 0), (wo, 1)], F32, n=d_model, res=xf, tn=1024, name="out_proj")

        h2 = rmsnorm(xf, norm_ffn[i], BF16)
        hid, wd = swiglu_up(h2, wg, wu, w_down[i])
        if i + 1 < depth:
            xf = matmul_k(hid, wd, xf, F32, name="ffn_down")
    y = matmul_k(hid, wd, None, BF16, tn=1024, name="ffn_down")
    out = rmsnorm(xf, norm_final, x.dtype, add=y)
    return out.reshape(bsz, seq, d_model)
```

```python
import functools

import jax
import jax.numpy as jnp
from jax import lax
from jax.experimental import pallas as pl
from jax.experimental.pallas import tpu as pltpu

F32 = jnp.float32
BF16 = jnp.bfloat16

ATT_HEADS = 16
HEAD_DIM = 128
IDX_HEADS = 32
IDX_DIM = 64
TOPK_MAX = 256
GMLP_GROUPS = 16
GMLP_CH = 128
CHUNK = 128
EPS = 1e-6
NEG = -1e30
BIG = 1e30
LOG2E = 1.4426950408889634

LANES = 128
F32_SUBLANES = 8
BF16_SUBLANES = 16
ONES_ROWS = BF16_SUBLANES
VMEM_LIMIT = 56 * 1024 * 1024
BISECT_MAX_ITERS = 320


def _cparams(sem):
    return pltpu.CompilerParams(dimension_semantics=sem, vmem_limit_bytes=VMEM_LIMIT)


def _rmsnorm_kernel(*refs, has_add):
    x_ref, g_ref, o_ref = refs[0], refs[-2], refs[-1]
    x = x_ref[...].astype(F32)
    if has_add:
        x = x + refs[1][...].astype(F32)
    ms = jnp.mean(x * x, axis=-1, keepdims=True)
    o_ref[...] = (x * lax.rsqrt(ms + EPS) * g_ref[...]).astype(o_ref.dtype)


def _rmsnorm_proj_kernel(x_ref, g_ref, w_ref, sc_ref, o_ref, p_ref):
    x = x_ref[...].astype(F32)
    ms = jnp.mean(x * x, axis=-1, keepdims=True)
    h = (x * lax.rsqrt(ms + EPS) * g_ref[...]).astype(o_ref.dtype)
    o_ref[...] = h
    wb = (w_ref[...] * sc_ref[...]).astype(BF16)
    p_ref[0, 0] = lax.dot_general(wb, h, (((1,), (1,)), ((), ())), preferred_element_type=F32)


def rmsnorm_proj_t(x, g, w_t, scale, row_block, seq, tm=512):
    m, d = x.shape
    tm = min(tm, m, seq)
    nmb = seq // tm
    row_spec = pl.BlockSpec((tm, d), lambda i: (i, 0))
    return pl.pallas_call(
        _rmsnorm_proj_kernel,
        grid=(m // tm,),
        in_specs=[row_spec,
                  pl.BlockSpec((1, d), lambda i: (0, 0)),
                  pl.BlockSpec((LANES, d), lambda i: (row_block, 0)),
                  pl.BlockSpec((LANES, 1), lambda i: (row_block, 0))],
        out_specs=[row_spec,
                   pl.BlockSpec((1, 1, LANES, tm), lambda i: (i // nmb, 0, 0, i % nmb))],
        out_shape=[jax.ShapeDtypeStruct((m, d), BF16),
                   jax.ShapeDtypeStruct((m // seq, 1, LANES, seq), F32)],
        compiler_params=_cparams(("parallel",)),
        name="rmsnorm_proj",
    )(x, g.reshape(1, d).astype(F32), w_t, scale)


def rmsnorm(x, g, out_dtype, add=None, tm=512):
    m, d = x.shape
    tm = min(tm, m)
    row_spec = pl.BlockSpec((tm, d), lambda i: (i, 0))
    ins = [x] if add is None else [x, add]
    return pl.pallas_call(
        functools.partial(_rmsnorm_kernel, has_add=add is not None),
        grid=(m // tm,),
        in_specs=[row_spec] * len(ins) + [pl.BlockSpec((1, d), lambda i: (0, 0))],
        out_specs=row_spec,
        out_shape=jax.ShapeDtypeStruct((m, d), out_dtype),
        compiler_params=_cparams(("parallel",)),
        name="rmsnorm",
    )(*ins, g.reshape(1, d).astype(F32))


def _matmul_kernel(*refs, n_pairs, has_scale, has_res, has_side, cast_w, w_t, out_mode,
                   post_scale):
    a_refs = refs[:n_pairs]
    w_refs = refs[n_pairs:2 * n_pairs]
    pos = 2 * n_pairs
    sc_ref = refs[pos] if has_scale else None
    pos += int(has_scale)
    r_ref = refs[pos] if has_res else None
    pos += int(has_res)
    side_ref = refs[pos] if has_side else None
    pos += int(has_side)
    o_ref = refs[pos]
    pos += 1
    if has_side:
        refs[pos][...] = side_ref[...].astype(BF16)
        pos += 1
    wb_refs = refs[pos:]

    if cast_w:
        @pl.when(pl.program_id(1) == 0)
        def _():
            for p in range(n_pairs):
                w = w_refs[p][...]
                if has_scale:
                    w = w * sc_ref[...]
                wb_refs[p][...] = w.astype(BF16)
        rhs = wb_refs
    else:
        rhs = w_refs

    nt = (((1,), (1,)), ((), ()))
    if out_mode in ("slabs_t", "slabs_t_ones"):
        acc = lax.dot_general(rhs[0][...], a_refs[0][...], nt, preferred_element_type=F32)
        if post_scale is not None:
            acc = acc * post_scale
        for pp in range(acc.shape[0] // LANES):
            o_ref[0, pp, :LANES, :] = acc[pp * LANES:(pp + 1) * LANES, :].astype(o_ref.dtype)
            if out_mode == "slabs_t_ones":
                o_ref[0, pp, LANES:, :] = jnp.ones((ONES_ROWS, acc.shape[1]), o_ref.dtype)
        return

    def mm(p):
        if w_t:
            return lax.dot_general(a_refs[p][...], rhs[p][...], nt, preferred_element_type=F32)
        return jnp.dot(a_refs[p][...], rhs[p][...], preferred_element_type=F32)

    acc = mm(0)
    for p in range(1, n_pairs):
        acc = acc + mm(p)
    if post_scale is not None:
        acc = acc * post_scale
    if has_res:
        acc = acc + r_ref[...]

    o_ref[...] = acc.astype(o_ref.dtype)


def matmul(a_list, w_list, out_dtype, *, n, col_off=0, w_t=False, scale=None, post_scale=None,
           res=None, side=None, side_rows=None, out_mode="plain", seq=None, tm=1024, tn=512,
           name="matmul"):
    n_pairs = len(a_list)
    m = a_list[0].shape[0]
    tm = min(tm, m)
    tn = min(tn, n)
    cast_w = w_list[0][0].dtype != BF16
    in_specs, args, scratch = [], [], []
    for a in a_list:
        in_specs.append(pl.BlockSpec((tm, a.shape[1]), lambda j, i: (i, 0)))
        args.append(a)
    for a, (w, rb) in zip(a_list, w_list):
        k = a.shape[1]
        if w_t:
            in_specs.append(pl.BlockSpec((tn, k), lambda j, i: (j + col_off, 0)))
        else:
            in_specs.append(pl.BlockSpec((k, tn), lambda j, i, rb=rb: (rb, j + col_off)))
        args.append(w)
        if cast_w:
            scratch.append(pltpu.VMEM((tn, k) if w_t else (k, tn), BF16))
    if scale is not None:
        assert w_t and cast_w
        in_specs.append(pl.BlockSpec((tn, 1), lambda j, i: (j + col_off, 0)))
        args.append(scale)
    if res is not None:
        in_specs.append(pl.BlockSpec((tm, tn), lambda j, i: (i, j)))
        args.append(res)
    nmb_all = m // tm
    if side is not None:
        side_spec, side_shape = _side_specs(side, (n // tn) * nmb_all,
                                            lambda j, i: j * nmb_all + i, rows=side_rows)
        in_specs.append(side_spec)
        args.append(side)
    if out_mode == "plain":
        out_spec = pl.BlockSpec((tm, tn), lambda j, i: (i, j))
        out_shape = jax.ShapeDtypeStruct((m, n), out_dtype)
    else:
        assert w_t and n_pairs == 1 and res is None
        nmb = seq // tm
        rows = LANES + (ONES_ROWS if out_mode == "slabs_t_ones" else 0)
        out_spec = pl.BlockSpec((1, tn // LANES, rows, tm), lambda j, i: (i // nmb, j, 0, i % nmb))
        out_shape = jax.ShapeDtypeStruct((m // seq, n // LANES, rows, seq), out_dtype)
    if side is not None:
        out_spec, out_shape = [out_spec, side_spec], [out_shape, side_shape]
    return pl.pallas_call(
        functools.partial(_matmul_kernel, n_pairs=n_pairs, has_scale=scale is not None,
                          has_res=res is not None, has_side=side is not None,
                          cast_w=cast_w, w_t=w_t, out_mode=out_mode, post_scale=post_scale),
        grid=(n // tn, m // tm),
        in_specs=in_specs,
        out_specs=out_spec,
        out_shape=out_shape,
        scratch_shapes=scratch,
        compiler_params=_cparams(("parallel", "arbitrary")),
        name=name,
    )(*args)


def _side_specs(w, steps, step_of, rows=None):
    cols = w.shape[1]
    rows = w.shape[0] if rows is None else rows
    slab = rows // steps
    assert slab * steps == rows and slab % BF16_SUBLANES == 0, (rows, steps)
    spec = pl.BlockSpec((slab, cols), lambda *g: (step_of(*g), 0))
    return spec, jax.ShapeDtypeStruct((rows, cols), BF16)


def _swiglu_up_kernel(h_ref, wg_ref, wu_ref, wd_ref, o_ref, wdb_ref):
    wdb_ref[...] = wd_ref[...].astype(BF16)
    h = h_ref[...]
    g = jnp.dot(h, wg_ref[...], preferred_element_type=F32)
    u = jnp.dot(h, wu_ref[...], preferred_element_type=F32)
    o_ref[...] = (g * jax.nn.sigmoid(g) * u).astype(o_ref.dtype)


def swiglu_up(h, wg, wu, wd, tm=2048, tn=256):
    m, k = h.shape
    n = wg.shape[1]
    tm = min(tm, m)
    tn = min(tn, n)
    nnb = n // tn
    side_spec, side_shape = _side_specs(wd, (m // tm) * nnb, lambda i, j: i * nnb + j)
    return pl.pallas_call(
        _swiglu_up_kernel,
        grid=(m // tm, nnb),
        in_specs=[pl.BlockSpec((tm, k), lambda i, j: (i, 0)),
                  pl.BlockSpec((k, tn), lambda i, j: (0, j)),
                  pl.BlockSpec((k, tn), lambda i, j: (0, j)),
                  side_spec],
        out_specs=[pl.BlockSpec((tm, tn), lambda i, j: (i, j)), side_spec],
        out_shape=[jax.ShapeDtypeStruct((m, n), BF16), side_shape],
        compiler_params=_cparams(("parallel", "parallel")),
        name="swiglu_up",
    )(h, wg, wu, wd)


def _matmul_k_kernel(*refs, has_res, nk):
    a_ref, b_ref = refs[0], refs[1]
    o_ref, acc_ref = refs[-2], refs[-1]
    kk = pl.program_id(2)
    part = jnp.dot(a_ref[...], b_ref[...], preferred_element_type=F32)
    if has_res:
        first = part + refs[2][...]
    else:
        first = part
    if nk == 1:
        o_ref[...] = first.astype(o_ref.dtype)
        return

    @pl.when(kk == 0)
    def _():
        acc_ref[...] = first

    @pl.when(jnp.logical_and(kk > 0, kk < nk - 1))
    def _():
        acc_ref[...] += part

    @pl.when(kk == nk - 1)
    def _():
        o_ref[...] = (acc_ref[...] + part).astype(o_ref.dtype)


def matmul_k(a, b, res, out_dtype, tm=1024, tn=512, tk=5504, name="matmul_k"):
    m, k = a.shape
    n = b.shape[1]
    tm = min(tm, m)
    tn = min(tn, n)
    tk = min(tk, k)
    in_specs = [pl.BlockSpec((tm, tk), lambda i, j, kk: (i, kk)),
                pl.BlockSpec((tk, tn), lambda i, j, kk: (kk, j))]
    args = [a, b]
    if res is not None:
        in_specs.append(pl.BlockSpec((tm, tn), lambda i, j, kk: (i, j)))
        args.append(res)
    return pl.pallas_call(
        functools.partial(_matmul_k_kernel, has_res=res is not None, nk=k // tk),
        grid=(m // tm, n // tn, k // tk),
        in_specs=in_specs,
        out_specs=pl.BlockSpec((tm, tn), lambda i, j, kk: (i, j)),
        out_shape=jax.ShapeDtypeStruct((m, n), out_dtype),
        scratch_shapes=[pltpu.VMEM((tm, tn), F32)],
        compiler_params=_cparams(("parallel", "parallel", "arbitrary")),
        name=name,
    )(*args)


def _select_kernel(q_ref, kw_ref, kwq_ref, sa_ref, sb_ref, bias_ref, sa16_ref, sb16_ref,
                   s_ref, ke_ref, ko_ref, *, tq, tk, n_chunks, topk, tie_iters):
    sa16_ref[...] = sa_ref[...].astype(BF16)
    sb16_ref[...] = sb_ref[...].astype(BF16)
    i = pl.program_id(1)

    @pl.when(i == 0)
    def _():
        lane = lax.broadcasted_iota(jnp.int32, (tk, LANES), 1)
        for c in range(n_chunks):
            kt = kw_ref[0, 0, :, c * tk:(c + 1) * tk].T
            ke_ref[c * tk:(c + 1) * tk, :] = jnp.where(lane < IDX_DIM, kt, 0.0).astype(BF16)
            ko_ref[c * tk:(c + 1) * tk, :] = jnp.where(
                lane >= IDX_DIM, pltpu.roll(kt, IDX_DIM, axis=1), 0.0).astype(BF16)

    t0 = i * tq
    nj = (t0 + tq + tk - 1) // tk
    qpos = t0 + lax.broadcasted_iota(jnp.int32, (1, tq), 1)
    w = kwq_ref[0, 0, IDX_DIM:IDX_DIM + IDX_HEADS, :]

    def score_chunk(j, carry):
        rmin, rmax = carry
        r0 = pl.multiple_of(j * tk, tk)
        ke = ke_ref[pl.ds(r0, tk), :]
        ko = ko_ref[pl.ds(r0, tk), :]
        acc = jnp.zeros((tk, tq), F32)
        for p in range(IDX_HEADS // 2):
            qp = q_ref[0, p]
            le = jnp.dot(ke, qp, preferred_element_type=F32)
            lo = jnp.dot(ko, qp, preferred_element_type=F32)
            acc = (acc + w[2 * p:2 * p + 1, :] * jnp.maximum(le, 0.0)
                   + w[2 * p + 1:2 * p + 2, :] * jnp.maximum(lo, 0.0))
        kpos = j * tk + lax.broadcasted_iota(jnp.int32, (tk, 1), 0)
        causal = kpos <= qpos
        s_ref[j] = jnp.where(causal, acc, NEG)
        rmax = jnp.maximum(rmax, jnp.max(jnp.where(causal, acc, NEG), axis=0, keepdims=True))
        rmin = jnp.minimum(rmin, jnp.min(jnp.where(causal, acc, BIG), axis=0, keepdims=True))
        return rmin, rmax

    rmin, rmax = lax.fori_loop(
        0, nj, score_chunk,
        (jnp.full((1, tq), BIG, F32), jnp.full((1, tq), NEG, F32)))

    n_causal = (qpos + 1).astype(F32)
    keff = jnp.minimum(n_causal, float(topk))
    nacc = 64

    def count_ge(thr):
        def body(j, c):
            part = jnp.where(s_ref[j] >= thr, 1.0, 0.0)
            return c + jnp.sum(part.reshape(tk // nacc, nacc, tq), axis=0)
        c = lax.fori_loop(0, nj, body, jnp.zeros((nacc, tq), F32))
        return jnp.sum(c, axis=0, keepdims=True)

    def any_true(x):
        return jnp.max(jnp.where(x, 1.0, 0.0)) > 0.0

    hi0 = rmax
    c_max = count_ge(hi0)
    at_max = c_max >= keff
    lo0 = jnp.where(at_max, hi0, rmin)
    c_lo0 = jnp.where(at_max, c_max, n_causal)
    c_hi0 = jnp.where(at_max, 0.0, c_max)

    def cond(st):
        it, lo, hi, c_lo, _ = st
        mid = 0.5 * (lo + hi)
        open_ = jnp.logical_and(c_lo != keff, jnp.logical_and(mid > lo, mid < hi))
        return jnp.logical_and(it < BISECT_MAX_ITERS, any_true(open_))

    def halve(lo, hi, c_lo, c_hi):
        mid = 0.5 * (lo + hi)
        c = count_ge(mid)
        ge = c >= keff
        return (jnp.where(ge, mid, lo), jnp.where(ge, hi, mid),
                jnp.where(ge, c, c_lo), jnp.where(ge, c_hi, c))

    def bisect(st):
        it, lo, hi, c_lo, c_hi = st
        lo, hi, c_lo, c_hi = halve(*halve(lo, hi, c_lo, c_hi))
        return it + 2, lo, hi, c_lo, c_hi

    _, lo, _, c_lo, c_hi = lax.while_loop(cond, bisect, (jnp.int32(0), lo0, hi0, c_lo0, c_hi0))

    tied = c_lo != keff
    want = keff - c_hi

    def count_tied_upto(jpos):
        def body(j, c):
            kposf = (j * tk + lax.broadcasted_iota(jnp.int32, (tk, 1), 0)).astype(F32)
            hit = jnp.logical_and(s_ref[j] == lo, kposf <= jpos)
            return c + jnp.sum(jnp.where(hit, 1.0, 0.0).reshape(tk // nacc, nacc, tq), axis=0)
        c = lax.fori_loop(0, nj, body, jnp.zeros((nacc, tq), F32))
        return jnp.sum(c, axis=0, keepdims=True)

    def tie_cond(st):
        it, jl, jh = st
        return jnp.logical_and(it < tie_iters, any_true(jnp.logical_and(tied, jh - jl > 1.0)))

    def tie_step(st):
        it, jl, jh = st
        jm = jnp.floor(0.5 * (jl + jh))
        ok = count_tied_upto(jm) >= want
        return it + 1, jnp.where(ok, jl, jm), jnp.where(ok, jm, jh)

    _, _, jh = lax.while_loop(tie_cond, tie_step,
                              (jnp.int32(0), jnp.full((1, tq), -1.0, F32), qpos.astype(F32)))
    jmax = jnp.where(tied, jh, BIG)
    any_tied = any_true(tied)

    def write_sel(j, c):
        r0 = pl.multiple_of(j * tk, tk)
        bias_ref[0, pl.ds(r0, tk), :] = jnp.where(s_ref[j] >= lo, 0.0, NEG).astype(bias_ref.dtype)
        return c

    def write_sel_tied(j, c):
        r0 = pl.multiple_of(j * tk, tk)
        s = s_ref[j]
        kposf = (j * tk + lax.broadcasted_iota(jnp.int32, (tk, 1), 0)).astype(F32)
        keep = jnp.logical_or(s > lo, jnp.logical_and(s == lo, kposf <= jmax))
        bias_ref[0, pl.ds(r0, tk), :] = jnp.where(keep, 0.0, NEG).astype(bias_ref.dtype)
        return c

    def write_none(j, c):
        r0 = pl.multiple_of(j * tk, tk)
        bias_ref[0, pl.ds(r0, tk), :] = jnp.full((tk, tq), NEG, bias_ref.dtype)
        return c

    @pl.when(jnp.logical_not(any_tied))
    def _():
        lax.fori_loop(0, nj, write_sel, 0)

    @pl.when(any_tied)
    def _():
        lax.fori_loop(0, nj, write_sel_tied, 0)

    lax.fori_loop(nj, n_chunks, write_none, 0)


def select_mask(qt, kw, side_a, side_b, *, tq, tk, topk):
    bsz, npair, _, seq = qt.shape
    n_chunks = seq // tk
    nqb = seq // tq
    step_of = lambda b, i: b * nqb + i
    spec_a, shape_a = _side_specs(side_a, bsz * nqb, step_of)
    spec_b, shape_b = _side_specs(side_b, bsz * nqb, step_of)
    return pl.pallas_call(
        functools.partial(_select_kernel, tq=tq, tk=tk, n_chunks=n_chunks, topk=topk,
                          tie_iters=seq.bit_length() + 1),
        grid=(bsz, nqb),
        in_specs=[pl.BlockSpec((1, npair, LANES, tq), lambda b, i: (b, 0, 0, i)),
                  pl.BlockSpec((1, 1, LANES, seq), lambda b, i: (b, 0, 0, 0)),
                  pl.BlockSpec((1, 1, LANES, tq), lambda b, i: (b, 0, 0, i)),
                  spec_a, spec_b],
        out_specs=[pl.BlockSpec((1, seq, tq), lambda b, i: (b, 0, i)), spec_a, spec_b],
        out_shape=[jax.ShapeDtypeStruct((bsz, seq, seq), BF16), shape_a, shape_b],
        scratch_shapes=[pltpu.VMEM((n_chunks, tk, tq), F32),
                        pltpu.VMEM((seq, LANES), BF16),
                        pltpu.VMEM((seq, LANES), BF16)],
        compiler_params=_cparams(("parallel", "arbitrary")),
        name="select_mask",
    )(qt, kw, kw, side_a, side_b)


def _attn_kernel(q_ref, k_ref, v_ref, b_ref, o_ref, m_ref, acc_ref, *, tq, tk, nh, dh):
    i = pl.program_id(1)
    j = pl.program_id(2)

    @pl.when(j == 0)
    def _():
        m_ref[...] = jnp.full(m_ref.shape, -jnp.inf, F32)
        acc_ref[...] = jnp.zeros(acc_ref.shape, F32)

    @pl.when(j * tk <= i * tq + tq - 1)
    def _():
        bias = b_ref[0]
        for h in range(nh):
            s = jnp.dot(k_ref[:, h * dh:(h + 1) * dh], q_ref[0, h], preferred_element_type=F32)
            sb = s.astype(BF16) + bias
            part = jnp.max(sb.reshape(tk // BF16_SUBLANES, BF16_SUBLANES, tq), axis=0)
            m_blk = jnp.max(part.astype(F32), axis=0, keepdims=True)
            m_old = m_ref[h]
            m_new = jnp.maximum(m_old, m_blk)
            p = jnp.exp2(sb - m_new[:1].astype(BF16))
            alpha = jnp.exp2(m_old - m_new)
            pv = jnp.dot(v_ref[0, h], p, preferred_element_type=F32)
            acc_ref[h] = alpha[:1] * acc_ref[h] + pv
            m_ref[h] = m_new

    @pl.when(j == pl.num_programs(2) - 1)
    def _():
        for h in range(nh):
            acc = acc_ref[h]
            o = acc[:dh] / acc[dh:dh + 1]
            o_ref[:, h * dh:(h + 1) * dh] = o.T.astype(o_ref.dtype)


def masked_attention(qt, kx, vt, bias, *, bsz, seq, tq, tk):
    nh, dh = ATT_HEADS, HEAD_DIM
    d_att = nh * dh
    nqb, nkb = seq // tq, seq // tk
    vrows = dh + ONES_ROWS

    def jc(i, j):
        return jnp.minimum(j, (i * tq + tq - 1) // tk)

    return pl.pallas_call(
        functools.partial(_attn_kernel, tq=tq, tk=tk, nh=nh, dh=dh),
        grid=(bsz, nqb, nkb),
        in_specs=[pl.BlockSpec((1, nh, dh, tq), lambda b, i, j: (b, 0, 0, i)),
                  pl.BlockSpec((tk, d_att), lambda b, i, j: (b * nkb + jc(i, j), 0)),
                  pl.BlockSpec((1, nh, vrows, tk), lambda b, i, j: (b, 0, 0, jc(i, j))),
                  pl.BlockSpec((1, tk, tq), lambda b, i, j: (b, jc(i, j), i))],
        out_specs=pl.BlockSpec((tq, d_att), lambda b, i, j: (b * nqb + i, 0)),
        out_shape=jax.ShapeDtypeStruct((bsz * seq, d_att), BF16),
        scratch_shapes=[pltpu.VMEM((nh, F32_SUBLANES, tq), F32),
                        pltpu.VMEM((nh, vrows, tq), F32)],
        compiler_params=_cparams(("parallel", "parallel", "arbitrary")),
        name="masked_attention",
    )(qt, kx, vt, bias)


def _gelu_tanh(x):
    return 0.5 * x * (1.0 + jnp.tanh(0.7978845608028654 * (x + 0.044715 * (x * x * x))))


def _gmlp_kernel(h_ref, wu_ref, wv_ref, gain_ref, w_ref, bias_ref, side_ref, o_ref, side16_ref,
                 *, n_sub, n_grp):
    side16_ref[...] = side_ref[...].astype(BF16)
    nt = (((1,), (1,)), ((), ()))
    h = h_ref[...]
    u_all = lax.dot_general(h, wu_ref[...], nt, preferred_element_type=F32)
    v_all = lax.dot_general(h, wv_ref[...], nt, preferred_element_type=F32)
    tri = (lax.broadcasted_iota(jnp.int32, (CHUNK, CHUNK), 1)
           <= lax.broadcasted_iota(jnp.int32, (CHUNK, CHUNK), 0))
    for g in range(n_grp):
        gs = slice(g * GMLP_CH, (g + 1) * GMLP_CH)
        wg = jnp.where(tri, w_ref[g], 0.0).astype(BF16)
        gain = gain_ref[:, gs]
        bias = bias_ref[:, gs]
        for c in range(n_sub):
            rs = slice(c * CHUNK, (c + 1) * CHUNK)
            u = _gelu_tanh(u_all[rs, gs])
            v = _gelu_tanh(v_all[rs, gs])
            mu = jnp.mean(v, axis=-1, keepdims=True)
            vc = v - mu
            var = jnp.mean(vc * vc, axis=-1, keepdims=True)
            vn = (vc * lax.rsqrt(var + EPS) * gain).astype(BF16)
            z = jnp.dot(wg, vn, preferred_element_type=F32) + bias
            o_ref[rs, gs] = (u * z).astype(o_ref.dtype)


def gmlp_mixer(h, w_uv_t, v_gain, w_s, b_s, side, tm=1024, n_grp=4):
    m, k = h.shape
    d = GMLP_GROUPS * GMLP_CH
    tm = min(tm, m)
    tn = n_grp * GMLP_CH
    ngb = d // tn
    nmb = m // tm
    gain = v_gain.reshape(1, d).astype(F32)
    bias_full = jnp.repeat(b_s.T.astype(F32), GMLP_CH, axis=1)
    side_spec, side_shape = _side_specs(side, ngb * nmb, lambda j, i: j * nmb + i)
    return pl.pallas_call(
        functools.partial(_gmlp_kernel, n_sub=tm // CHUNK, n_grp=n_grp),
        grid=(ngb, nmb),
        in_specs=[pl.BlockSpec((tm, k), lambda j, i: (i, 0)),
                  pl.BlockSpec((tn, k), lambda j, i: (j, 0)),
                  pl.BlockSpec((tn, k), lambda j, i: (j + ngb, 0)),
                  pl.BlockSpec((1, tn), lambda j, i: (0, j)),
                  pl.BlockSpec((n_grp, CHUNK, CHUNK), lambda j, i: (j, 0, 0)),
                  pl.BlockSpec((CHUNK, tn), lambda j, i: (0, j)),
                  side_spec],
        out_specs=[pl.BlockSpec((tm, tn), lambda j, i: (i, j)), side_spec],
        out_shape=[jax.ShapeDtypeStruct((m, d), BF16), side_shape],
        compiler_params=_cparams(("parallel", "parallel")),
        name="gmlp_mixer",
    )(h, w_uv_t, w_uv_t, gain, w_s.astype(F32), bias_full, side)


def dsa_attention(qt, kx, vt, qit, kw, side_a, side_b, *, bsz, seq,
                  tq_sel=256, tq_att=1024, tk_sel=512, tk_att=1024):
    tq_sel, tk_sel = min(tq_sel, seq), min(tk_sel, seq)
    tq_att, tk_att = min(tq_att, seq), min(tk_att, seq)
    topk = min(TOPK_MAX, seq // 4)
    bias, side_a16, side_b16 = select_mask(qit, kw, side_a, side_b,
                                           tq=tq_sel, tk=tk_sel, topk=topk)
    att = masked_attention(qt, kx, vt, bias, bsz=bsz, seq=seq, tq=tq_att, tk=tk_att)
    return att, side_a16, side_b16


def _in_proj_scale(d_att, n_idx, d_gm):
    return jnp.concatenate([
        jnp.full((d_att,), LOG2E * HEAD_DIM ** -0.5, F32),
        jnp.ones((2 * d_att,), F32),
        jnp.full((n_idx,), IDX_DIM ** -0.5, F32),
        jnp.ones((IDX_DIM,), F32),
        jnp.full((IDX_HEADS,), IDX_HEADS ** -0.5, F32),
        jnp.ones((2 * d_gm,), F32)]).reshape(-1, 1)


def kernel(x, norm_mix, w_in, gmlp_v_gain, w_spatial, b_spatial, w_out, norm_ffn,
           w_gate, w_up, w_down, norm_final):
    bsz, seq, d_model = x.shape
    m = bsz * seq
    d_att = ATT_HEADS * HEAD_DIM
    d_gm = GMLP_GROUPS * GMLP_CH
    n_idx = IDX_HEADS * IDX_DIM
    depth = w_in.shape[0]
    tn = 512
    scale = _in_proj_scale(d_att, n_idx, d_gm)
    xf = x.reshape(m, d_model)
    for i in range(depth):
        wt = jnp.swapaxes(w_in[i], 0, 1)
        uv_off = 3 * d_att + n_idx + IDX_DIM + IDX_HEADS
        w_uv = wt[uv_off:].astype(BF16)

        h, kw = rmsnorm_proj_t(xf, norm_mix[i], wt, scale, (3 * d_att + n_idx) // LANES, seq)
        n_qkvi = 3 * d_att + n_idx
        qt, w16 = matmul([h], [(wt, 0)], BF16, n=d_att, col_off=0, w_t=True, scale=scale,
                         side=wt, side_rows=n_qkvi, out_mode="slabs_t", seq=seq, tn=tn,
                         name="proj_q")
        tnb = 1024
        kx = matmul([h], [(w16, 0)], BF16, n=d_att, col_off=d_att // tnb, w_t=True,
                    tn=tnb, name="proj_k")
        vt = matmul([h], [(w16, 0)], BF16, n=d_att, col_off=2 * d_att // tnb, w_t=True,
                    out_mode="slabs_t_ones", seq=seq, tn=tnb, name="proj_v")
        qit = matmul([h], [(w16, 0)], BF16, n=n_idx, col_off=3 * d_att // tnb, w_t=True,
                     post_scale=IDX_DIM ** -0.5, out_mode="slabs_t", seq=seq, tn=tnb,
                     name="proj_qidx")
        gm, wo = gmlp_mixer(h, w_uv, gmlp_v_gain[i], w_spatial[i], b_spatial[i], w_out[i])

        att, wg, wu = dsa_attention(qt, kx, vt, qit, kw, w_gate[i], w_up[i], bsz=bsz, seq=seq)

        xf = matmul([att, gm], [(wo, 0), (wo, 1)], F32, n=d_model, res=xf, tn=1024, name="out_proj")

        h2 = rmsnorm(xf, norm_ffn[i], BF16)
        hid, wd = swiglu_up(h2, wg, wu, w_down[i])
        if i + 1 < depth:
            xf = matmul_k(hid, wd, xf, F32, name="ffn_down")
    y = matmul_k(hid, wd, None, BF16, tn=1024, name="ffn_down")
    out = rmsnorm(xf, norm_final, x.dtype, add=y)
    return out.reshape(bsz, seq, d_model)
```

```python
import functools

import jax
import jax.numpy as jnp
from jax import lax
from jax.experimental import pallas as pl
from jax.experimental.pallas import tpu as pltpu

F32 = jnp.float32
BF16 = jnp.bfloat16

ATT_HEADS = 16
HEAD_DIM = 128
IDX_HEADS = 32
IDX_DIM = 64
TOPK_MAX = 256
GMLP_GROUPS = 16
GMLP_CH = 128
CHUNK = 128
EPS = 1e-6
NEG = -1e30
BIG = 1e30
LOG2E = 1.4426950408889634

LANES = 128
F32_SUBLANES = 8
BF16_SUBLANES = 16
ONES_ROWS = BF16_SUBLANES
VMEM_LIMIT = 56 * 1024 * 1024
BISECT_MAX_ITERS = 320


def _cparams(sem):
    return pltpu.CompilerParams(dimension_semantics=sem, vmem_limit_bytes=VMEM_LIMIT)


def _rmsnorm_kernel(*refs, has_add):
    x_ref, g_ref, o_ref = refs[0], refs[-2], refs[-1]
    x = x_ref[...].astype(F32)
    if has_add:
        x = x + refs[1][...].astype(F32)
    ms = jnp.mean(x * x, axis=-1, keepdims=True)
    o_ref[...] = (x * lax.rsqrt(ms + EPS) * g_ref[...]).astype(o_ref.dtype)


def _rmsnorm_proj_kernel(x_ref, g_ref, w_ref, sc_ref, o_ref, p_ref):
    x = x_ref[...].astype(F32)
    ms = jnp.mean(x * x, axis=-1, keepdims=True)
    h = (x * lax.rsqrt(ms + EPS) * g_ref[...]).astype(o_ref.dtype)
    o_ref[...] = h
    wb = (w_ref[...] * sc_ref[...]).astype(BF16)
    p_ref[0, 0] = lax.dot_general(wb, h, (((1,), (1,)), ((), ())), preferred_element_type=F32)


def rmsnorm_proj_t(x, g, w_t, scale, row_block, seq, tm=512):
    m, d = x.shape
    tm = min(tm, m, seq)
    nmb = seq // tm
    row_spec = pl.BlockSpec((tm, d), lambda i: (i, 0))
    return pl.pallas_call(
        _rmsnorm_proj_kernel,
        grid=(m // tm,),
        in_specs=[row_spec,
                  pl.BlockSpec((1, d), lambda i: (0, 0)),
                  pl.BlockSpec((LANES, d), lambda i: (row_block, 0)),
                  pl.BlockSpec((LANES, 1), lambda i: (row_block, 0))],
        out_specs=[row_spec,
                   pl.BlockSpec((1, 1, LANES, tm), lambda i: (i // nmb, 0, 0, i % nmb))],
        out_shape=[jax.ShapeDtypeStruct((m, d), BF16),
                   jax.ShapeDtypeStruct((m // seq, 1, LANES, seq), F32)],
        compiler_params=_cparams(("parallel",)),
        name="rmsnorm_proj",
    )(x, g.reshape(1, d).astype(F32), w_t, scale)


def rmsnorm(x, g, out_dtype, add=None, tm=512):
    m, d = x.shape
    tm = min(tm, m)
    row_spec = pl.BlockSpec((tm, d), lambda i: (i, 0))
    ins = [x] if add is None else [x, add]
    return pl.pallas_call(
        functools.partial(_rmsnorm_kernel, has_add=add is not None),
        grid=(m // tm,),
        in_specs=[row_spec] * len(ins) + [pl.BlockSpec((1, d), lambda i: (0, 0))],
        out_specs=row_spec,
        out_shape=jax.ShapeDtypeStruct((m, d), out_dtype),
        compiler_params=_cparams(("parallel",)),
        name="rmsnorm",
    )(*ins, g.reshape(1, d).astype(F32))


def _matmul_kernel(*refs, n_pairs, has_scale, has_res, has_side, cast_w, w_t, out_mode,
                   post_scale):
    a_refs = refs[:n_pairs]
    w_refs = refs[n_pairs:2 * n_pairs]
    pos = 2 * n_pairs
    sc_ref = refs[pos] if has_scale else None
    pos += int(has_scale)
    r_ref = refs[pos] if has_res else None
    pos += int(has_res)
    side_ref = refs[pos] if has_side else None
    pos += int(has_side)
    o_ref = refs[pos]
    pos += 1
    if has_side:
        refs[pos][...] = side_ref[...].astype(BF16)
        pos += 1
    wb_refs = refs[pos:]

    if cast_w:
        @pl.when(pl.program_id(1) == 0)
        def _():
            for p in range(n_pairs):
                w = w_refs[p][...]
                if has_scale:
                    w = w * sc_ref[...]
                wb_refs[p][...] = w.astype(BF16)
        rhs = wb_refs
    else:
        rhs = w_refs

    nt = (((1,), (1,)), ((), ()))
    if out_mode in ("slabs_t", "slabs_t_ones"):
        acc = lax.dot_general(rhs[0][...], a_refs[0][...], nt, preferred_element_type=F32)
        if post_scale is not None:
            acc = acc * post_scale
        for pp in range(acc.shape[0] // LANES):
            o_ref[0, pp, :LANES, :] = acc[pp * LANES:(pp + 1) * LANES, :].astype(o_ref.dtype)
            if out_mode == "slabs_t_ones":
                o_ref[0, pp, LANES:, :] = jnp.ones((ONES_ROWS, acc.shape[1]), o_ref.dtype)
        return

    def mm(p):
        if w_t:
            return lax.dot_general(a_refs[p][...], rhs[p][...], nt, preferred_element_type=F32)
        return jnp.dot(a_refs[p][...], rhs[p][...], preferred_element_type=F32)

    acc = mm(0)
    for p in range(1, n_pairs):
        acc = acc + mm(p)
    if post_scale is not None:
        acc = acc * post_scale
    if has_res:
        acc = acc + r_ref[...]

    o_ref[...] = acc.astype(o_ref.dtype)


def matmul(a_list, w_list, out_dtype, *, n, col_off=0, w_t=False, scale=None, post_scale=None,
           res=None, side=None, side_rows=None, out_mode="plain", seq=None, tm=1024, tn=512,
           name="matmul"):
    n_pairs = len(a_list)
    m = a_list[0].shape[0]
    tm = min(tm, m)
    tn = min(tn, n)
    cast_w = w_list[0][0].dtype != BF16
    in_specs, args, scratch = [], [], []
    for a in a_list:
        in_specs.append(pl.BlockSpec((tm, a.shape[1]), lambda j, i: (i, 0)))
        args.append(a)
    for a, (w, rb) in zip(a_list, w_list):
        k = a.shape[1]
        if w_t:
            in_specs.append(pl.BlockSpec((tn, k), lambda j, i: (j + col_off, 0)))
        else:
            in_specs.append(pl.BlockSpec((k, tn), lambda j, i, rb=rb: (rb, j + col_off)))
        args.append(w)
        if cast_w:
            scratch.append(pltpu.VMEM((tn, k) if w_t else (k, tn), BF16))
    if scale is not None:
        assert w_t and cast_w
        in_specs.append(pl.BlockSpec((tn, 1), lambda j, i: (j + col_off, 0)))
        args.append(scale)
    if res is not None:
        in_specs.append(pl.BlockSpec((tm, tn), lambda j, i: (i, j)))
        args.append(res)
    nmb_all = m // tm
    if side is not None:
        side_spec, side_shape = _side_specs(side, (n // tn) * nmb_all,
                                            lambda j, i: j * nmb_all + i, rows=side_rows)
        in_specs.append(side_spec)
        args.append(side)
    if out_mode == "plain":
        out_spec = pl.BlockSpec((tm, tn), lambda j, i: (i, j))
        out_shape = jax.ShapeDtypeStruct((m, n), out_dtype)
    else:
        assert w_t and n_pairs == 1 and res is None
        nmb = seq // tm
        rows = LANES + (ONES_ROWS if out_mode == "slabs_t_ones" else 0)
        out_spec = pl.BlockSpec((1, tn // LANES, rows, tm), lambda j, i: (i // nmb, j, 0, i % nmb))
        out_shape = jax.ShapeDtypeStruct((m // seq, n // LANES, rows, seq), out_dtype)
    if side is not None:
        out_spec, out_shape = [out_spec, side_spec], [out_shape, side_shape]
    return pl.pallas_call(
        functools.partial(_matmul_kernel, n_pairs=n_pairs, has_scale=scale is not None,
                          has_res=res is not None, has_side=side is not None,
                          cast_w=cast_w, w_t=w_t, out_mode=out_mode, post_scale=post_scale),
        grid=(n // tn, m // tm),
        in_specs=in_specs,
        out_specs=out_spec,
        out_shape=out_shape,
        scratch_shapes=scratch,
        compiler_params=_cparams(("parallel", "arbitrary")),
        name=name,
    )(*args)


def _side_specs(w, steps, step_of, rows=None):
    cols = w.shape[1]
    rows = w.shape[0] if rows is None else rows
    slab = rows // steps
    assert slab * steps == rows and slab % BF16_SUBLANES == 0, (rows, steps)
    spec = pl.BlockSpec((slab, cols), lambda *g: (step_of(*g), 0))
    return spec, jax.ShapeDtypeStruct((rows, cols), BF16)


def _swiglu_up_kernel(h_ref, wg_ref, wu_ref, wd_ref, o_ref, wdb_ref):
    wdb_ref[...] = wd_ref[...].astype(BF16)
    h = h_ref[...]
    g = jnp.dot(h, wg_ref[...], preferred_element_type=F32)
    u = jnp.dot(h, wu_ref[...], preferred_element_type=F32)
    o_ref[...] = ((0.5 * g) * (1.0 + jnp.tanh(0.5 * g)) * u).astype(o_ref.dtype)


def swiglu_up(h, wg, wu, wd, tm=2048, tn=256):
    m, k = h.shape
    n = wg.shape[1]
    tm = min(tm, m)
    tn = min(tn, n)
    nnb = n // tn
    side_spec, side_shape = _side_specs(wd, (m // tm) * nnb, lambda i, j: i * nnb + j)
    return pl.pallas_call(
        _swiglu_up_kernel,
        grid=(m // tm, nnb),
        in_specs=[pl.BlockSpec((tm, k), lambda i, j: (i, 0)),
                  pl.BlockSpec((k, tn), lambda i, j: (0, j)),
                  pl.BlockSpec((k, tn), lambda i, j: (0, j)),
                  side_spec],
        out_specs=[pl.BlockSpec((tm, tn), lambda i, j: (i, j)), side_spec],
        out_shape=[jax.ShapeDtypeStruct((m, n), BF16), side_shape],
        compiler_params=_cparams(("parallel", "parallel")),
        name="swiglu_up",
    )(h, wg, wu, wd)


def _matmul_k_kernel(*refs, has_res, nk):
    a_ref, b_ref = refs[0], refs[1]
    o_ref, acc_ref = refs[-2], refs[-1]
    kk = pl.program_id(2)
    part = jnp.dot(a_ref[...], b_ref[...], preferred_element_type=F32)
    if has_res:
        first = part + refs[2][...]
    else:
        first = part
    if nk == 1:
        o_ref[...] = first.astype(o_ref.dtype)
        return

    @pl.when(kk == 0)
    def _():
        acc_ref[...] = first

    @pl.when(jnp.logical_and(kk > 0, kk < nk - 1))
    def _():
        acc_ref[...] += part

    @pl.when(kk == nk - 1)
    def _():
        o_ref[...] = (acc_ref[...] + part).astype(o_ref.dtype)


def matmul_k(a, b, res, out_dtype, tm=1024, tn=512, tk=5504, name="matmul_k"):
    m, k = a.shape
    n = b.shape[1]
    tm = min(tm, m)
    tn = min(tn, n)
    tk = min(tk, k)
    in_specs = [pl.BlockSpec((tm, tk), lambda i, j, kk: (i, kk)),
                pl.BlockSpec((tk, tn), lambda i, j, kk: (kk, j))]
    args = [a, b]
    if res is not None:
        in_specs.append(pl.BlockSpec((tm, tn), lambda i, j, kk: (i, j)))
        args.append(res)
    return pl.pallas_call(
        functools.partial(_matmul_k_kernel, has_res=res is not None, nk=k // tk),
        grid=(m // tm, n // tn, k // tk),
        in_specs=in_specs,
        out_specs=pl.BlockSpec((tm, tn), lambda i, j, kk: (i, j)),
        out_shape=jax.ShapeDtypeStruct((m, n), out_dtype),
        scratch_shapes=[pltpu.VMEM((tm, tn), F32)],
        compiler_params=_cparams(("parallel", "parallel", "arbitrary")),
        name=name,
    )(*args)


def _select_kernel(q_ref, kw_ref, kwq_ref, sa_ref, sb_ref, bias_ref, sa16_ref, sb16_ref,
                   s_ref, ke_ref, ko_ref, *, tq, tk, n_chunks, topk, tie_iters):
    sa16_ref[...] = sa_ref[...].astype(BF16)
    sb16_ref[...] = sb_ref[...].astype(BF16)
    i = pl.program_id(1)

    @pl.when(i == 0)
    def _():
        lane = lax.broadcasted_iota(jnp.int32, (tk, LANES), 1)
        for c in range(n_chunks):
            kt = kw_ref[0, 0, :, c * tk:(c + 1) * tk].T
            ke_ref[c * tk:(c + 1) * tk, :] = jnp.where(lane < IDX_DIM, kt, 0.0).astype(BF16)
            ko_ref[c * tk:(c + 1) * tk, :] = jnp.where(
                lane >= IDX_DIM, pltpu.roll(kt, IDX_DIM, axis=1), 0.0).astype(BF16)

    t0 = i * tq
    nj = (t0 + tq + tk - 1) // tk
    qpos = t0 + lax.broadcasted_iota(jnp.int32, (1, tq), 1)
    w = kwq_ref[0, 0, IDX_DIM:IDX_DIM + IDX_HEADS, :]

    def score_chunk(j, carry):
        rmin, rmax = carry
        r0 = pl.multiple_of(j * tk, tk)
        ke = ke_ref[pl.ds(r0, tk), :]
        ko = ko_ref[pl.ds(r0, tk), :]
        acc = jnp.zeros((tk, tq), F32)
        for p in range(IDX_HEADS // 2):
            qp = q_ref[0, p]
            le = jnp.dot(ke, qp, preferred_element_type=F32)
            lo = jnp.dot(ko, qp, preferred_element_type=F32)
            acc = (acc + w[2 * p:2 * p + 1, :] * jnp.maximum(le, 0.0)
                   + w[2 * p + 1:2 * p + 2, :] * jnp.maximum(lo, 0.0))
        kpos = j * tk + lax.broadcasted_iota(jnp.int32, (tk, 1), 0)
        causal = kpos <= qpos
        s_ref[j] = jnp.where(causal, acc, NEG)
        rmax = jnp.maximum(rmax, jnp.max(jnp.where(causal, acc, NEG), axis=0, keepdims=True))
        rmin = jnp.minimum(rmin, jnp.min(jnp.where(causal, acc, BIG), axis=0, keepdims=True))
        return rmin, rmax

    rmin, rmax = lax.fori_loop(
        0, nj, score_chunk,
        (jnp.full((1, tq), BIG, F32), jnp.full((1, tq), NEG, F32)))

    n_causal = (qpos + 1).astype(F32)
    keff = jnp.minimum(n_causal, float(topk))
    nacc = 64

    def count_ge(thr):
        def body(j, c):
            part = jnp.where(s_ref[j] >= thr, 1.0, 0.0)
            return c + jnp.sum(part.reshape(tk // nacc, nacc, tq), axis=0)
        c = lax.fori_loop(0, nj, body, jnp.zeros((nacc, tq), F32))
        return jnp.sum(c, axis=0, keepdims=True)

    def any_true(x):
        return jnp.max(jnp.where(x, 1.0, 0.0)) > 0.0

    hi0 = rmax
    c_max = count_ge(hi0)
    at_max = c_max >= keff
    lo0 = jnp.where(at_max, hi0, rmin)
    c_lo0 = jnp.where(at_max, c_max, n_causal)
    c_hi0 = jnp.where(at_max, 0.0, c_max)

    def cond(st):
        it, lo, hi, c_lo, _ = st
        mid = 0.5 * (lo + hi)
        open_ = jnp.logical_and(c_lo != keff, jnp.logical_and(mid > lo, mid < hi))
        return jnp.logical_and(it < BISECT_MAX_ITERS, any_true(open_))

    def halve(lo, hi, c_lo, c_hi):
        mid = 0.5 * (lo + hi)
        c = count_ge(mid)
        ge = c >= keff
        return (jnp.where(ge, mid, lo), jnp.where(ge, hi, mid),
                jnp.where(ge, c, c_lo), jnp.where(ge, c_hi, c))

    def bisect(st):
        it, lo, hi, c_lo, c_hi = st
        lo, hi, c_lo, c_hi = halve(*halve(lo, hi, c_lo, c_hi))
        return it + 2, lo, hi, c_lo, c_hi

    _, lo, _, c_lo, c_hi = lax.while_loop(cond, bisect, (jnp.int32(0), lo0, hi0, c_lo0, c_hi0))

    tied = c_lo != keff
    want = keff - c_hi

    def count_tied_upto(jpos):
        def body(j, c):
            kposf = (j * tk + lax.broadcasted_iota(jnp.int32, (tk, 1), 0)).astype(F32)
            hit = jnp.logical_and(s_ref[j] == lo, kposf <= jpos)
            return c + jnp.sum(jnp.where(hit, 1.0, 0.0).reshape(tk // nacc, nacc, tq), axis=0)
        c = lax.fori_loop(0, nj, body, jnp.zeros((nacc, tq), F32))
        return jnp.sum(c, axis=0, keepdims=True)

    def tie_cond(st):
        it, jl, jh = st
        return jnp.logical_and(it < tie_iters, any_true(jnp.logical_and(tied, jh - jl > 1.0)))

    def tie_step(st):
        it, jl, jh = st
        jm = jnp.floor(0.5 * (jl + jh))
        ok = count_tied_upto(jm) >= want
        return it + 1, jnp.where(ok, jl, jm), jnp.where(ok, jm, jh)

    _, _, jh = lax.while_loop(tie_cond, tie_step,
                              (jnp.int32(0), jnp.full((1, tq), -1.0, F32), qpos.astype(F32)))
    jmax = jnp.where(tied, jh, BIG)
    any_tied = any_true(tied)

    def write_sel(j, c):
        r0 = pl.multiple_of(j * tk, tk)
        bias_ref[0, pl.ds(r0, tk), :] = jnp.where(s_ref[j] >= lo, 0.0, NEG).astype(bias_ref.dtype)
        return c

    def write_sel_tied(j, c):
        r0 = pl.multiple_of(j * tk, tk)
        s = s_ref[j]
        kposf = (j * tk + lax.broadcasted_iota(jnp.int32, (tk, 1), 0)).astype(F32)
        keep = jnp.logical_or(s > lo, jnp.logical_and(s == lo, kposf <= jmax))
        bias_ref[0, pl.ds(r0, tk), :] = jnp.where(keep, 0.0, NEG).astype(bias_ref.dtype)
        return c

    def write_none(j, c):
        r0 = pl.multiple_of(j * tk, tk)
        bias_ref[0, pl.ds(r0, tk), :] = jnp.full((tk, tq), NEG, bias_ref.dtype)
        return c

    @pl.when(jnp.logical_not(any_tied))
    def _():
        lax.fori_loop(0, nj, write_sel, 0)

    @pl.when(any_tied)
    def _():
        lax.fori_loop(0, nj, write_sel_tied, 0)

    lax.fori_loop(nj, n_chunks, write_none, 0)


def select_mask(qt, kw, side_a, side_b, *, tq, tk, topk):
    bsz, npair, _, seq = qt.shape
    n_chunks = seq // tk
    nqb = seq // tq
    step_of = lambda b, i: b * nqb + i
    spec_a, shape_a = _side_specs(side_a, bsz * nqb, step_of)
    spec_b, shape_b = _side_specs(side_b, bsz * nqb, step_of)
    return pl.pallas_call(
        functools.partial(_select_kernel, tq=tq, tk=tk, n_chunks=n_chunks, topk=topk,
                          tie_iters=seq.bit_length() + 1),
        grid=(bsz, nqb),
        in_specs=[pl.BlockSpec((1, npair, LANES, tq), lambda b, i: (b, 0, 0, i)),
                  pl.BlockSpec((1, 1, LANES, seq), lambda b, i: (b, 0, 0, 0)),
                  pl.BlockSpec((1, 1, LANES, tq), lambda b, i: (b, 0, 0, i)),
                  spec_a, spec_b],
        out_specs=[pl.BlockSpec((1, seq, tq), lambda b, i: (b, 0, i)), spec_a, spec_b],
        out_shape=[jax.ShapeDtypeStruct((bsz, seq, seq), BF16), shape_a, shape_b],
        scratch_shapes=[pltpu.VMEM((n_chunks, tk, tq), F32),
                        pltpu.VMEM((seq, LANES), BF16),
                        pltpu.VMEM((seq, LANES), BF16)],
        compiler_params=_cparams(("parallel", "arbitrary")),
        name="select_mask",
    )(qt, kw, kw, side_a, side_b)


def _attn_kernel(q_ref, k_ref, v_ref, b_ref, o_ref, m_ref, acc_ref, *, tq, tk, nh, dh):
    i = pl.program_id(1)
    j = pl.program_id(2)

    @pl.when(j == 0)
    def _():
        m_ref[...] = jnp.full(m_ref.shape, -jnp.inf, F32)
        acc_ref[...] = jnp.zeros(acc_ref.shape, F32)

    @pl.when(j * tk <= i * tq + tq - 1)
    def _():
        bias = b_ref[0]
        for h in range(nh):
            s = jnp.dot(k_ref[:, h * dh:(h + 1) * dh], q_ref[0, h], preferred_element_type=F32)
            sb = s.astype(BF16) + bias
            part = jnp.max(sb.reshape(tk // BF16_SUBLANES, BF16_SUBLANES, tq), axis=0)
            m_blk = jnp.max(part.astype(F32), axis=0, keepdims=True)
            m_old = m_ref[h]
            m_new = jnp.maximum(m_old, m_blk)
            p = jnp.exp2(sb - m_new[:1].astype(BF16))
            alpha = jnp.exp2(m_old - m_new)
            pv = jnp.dot(v_ref[0, h], p, preferred_element_type=F32)
            acc_ref[h] = alpha[:1] * acc_ref[h] + pv
            m_ref[h] = m_new

    @pl.when(j == pl.num_programs(2) - 1)
    def _():
        for h in range(nh):
            acc = acc_ref[h]
            o = acc[:dh] / acc[dh:dh + 1]
            o_ref[:, h * dh:(h + 1) * dh] = o.T.astype(o_ref.dtype)


def masked_attention(qt, kx, vt, bias, *, bsz, seq, tq, tk):
    nh, dh = ATT_HEADS, HEAD_DIM
    d_att = nh * dh
    nqb, nkb = seq // tq, seq // tk
    vrows = dh + ONES_ROWS

    def jc(i, j):
        return jnp.minimum(j, (i * tq + tq - 1) // tk)

    return pl.pallas_call(
        functools.partial(_attn_kernel, tq=tq, tk=tk, nh=nh, dh=dh),
        grid=(bsz, nqb, nkb),
        in_specs=[pl.BlockSpec((1, nh, dh, tq), lambda b, i, j: (b, 0, 0, i)),
                  pl.BlockSpec((tk, d_att), lambda b, i, j: (b * nkb + jc(i, j), 0)),
                  pl.BlockSpec((1, nh, vrows, tk), lambda b, i, j: (b, 0, 0, jc(i, j))),
                  pl.BlockSpec((1, tk, tq), lambda b, i, j: (b, jc(i, j), i))],
        out_specs=pl.BlockSpec((tq, d_att), lambda b, i, j: (b * nqb + i, 0)),
        out_shape=jax.ShapeDtypeStruct((bsz * seq, d_att), BF16),
        scratch_shapes=[pltpu.VMEM((nh, F32_SUBLANES, tq), F32),
                        pltpu.VMEM((nh, vrows, tq), F32)],
        compiler_params=_cparams(("parallel", "parallel", "arbitrary")),
        name="masked_attention",
    )(qt, kx, vt, bias)


def _gelu_tanh(x):
    return 0.5 * x * (1.0 + jnp.tanh(0.7978845608028654 * (x + 0.044715 * (x * x * x))))


def _gmlp_kernel(h_ref, wu_ref, wv_ref, gain_ref, w_ref, bias_ref, side_ref, o_ref, side16_ref,
                 *, n_sub, n_grp):
    side16_ref[...] = side_ref[...].astype(BF16)
    nt = (((1,), (1,)), ((), ()))
    h = h_ref[...]
    u_all = lax.dot_general(h, wu_ref[...], nt, preferred_element_type=F32)
    v_all = lax.dot_general(h, wv_ref[...], nt, preferred_element_type=F32)
    tri = (lax.broadcasted_iota(jnp.int32, (CHUNK, CHUNK), 1)
           <= lax.broadcasted_iota(jnp.int32, (CHUNK, CHUNK), 0))
    for g in range(n_grp):
        gs = slice(g * GMLP_CH, (g + 1) * GMLP_CH)
        wg = jnp.where(tri, w_ref[g], 0.0).astype(BF16)
        gain = gain_ref[:, gs]
        bias = bias_ref[:, gs]
        for c in range(n_sub):
            rs = slice(c * CHUNK, (c + 1) * CHUNK)
            u = _gelu_tanh(u_all[rs, gs])
            v = _gelu_tanh(v_all[rs, gs])
            mu = jnp.mean(v, axis=-1, keepdims=True)
            vc = v - mu
            var = jnp.mean(vc * vc, axis=-1, keepdims=True)
            vn = (vc * lax.rsqrt(var + EPS) * gain).astype(BF16)
            z = jnp.dot(wg, vn, preferred_element_type=F32) + bias
            o_ref[rs, gs] = (u * z).astype(o_ref.dtype)


def gmlp_mixer(h, w_uv_t, v_gain, w_s, b_s, side, tm=1024, n_grp=4):
    m, k = h.shape
    d = GMLP_GROUPS * GMLP_CH
    tm = min(tm, m)
    tn = n_grp * GMLP_CH
    ngb = d // tn
    nmb = m // tm
    gain = v_gain.reshape(1, d).astype(F32)
    bias_full = jnp.repeat(b_s.T.astype(F32), GMLP_CH, axis=1)
    side_spec, side_shape = _side_specs(side, ngb * nmb, lambda j, i: j * nmb + i)
    return pl.pallas_call(
        functools.partial(_gmlp_kernel, n_sub=tm // CHUNK, n_grp=n_grp),
        grid=(ngb, nmb),
        in_specs=[pl.BlockSpec((tm, k), lambda j, i: (i, 0)),
                  pl.BlockSpec((tn, k), lambda j, i: (j, 0)),
                  pl.BlockSpec((tn, k), lambda j, i: (j + ngb, 0)),
                  pl.BlockSpec((1, tn), lambda j, i: (0, j)),
                  pl.BlockSpec((n_grp, CHUNK, CHUNK), lambda j, i: (j, 0, 0)),
                  pl.BlockSpec((CHUNK, tn), lambda j, i: (0, j)),
                  side_spec],
        out_specs=[pl.BlockSpec((tm, tn), lambda j, i: (i, j)), side_spec],
        out_shape=[jax.ShapeDtypeStruct((m, d), BF16), side_shape],
        compiler_params=_cparams(("parallel", "parallel")),
        name="gmlp_mixer",
    )(h, w_uv_t, w_uv_t, gain, w_s.astype(F32), bias_full, side)


def dsa_attention(qt, kx, vt, qit, kw, side_a, side_b, *, bsz, seq,
                  tq_sel=256, tq_att=1024, tk_sel=512, tk_att=1024):
    tq_sel, tk_sel = min(tq_sel, seq), min(tk_sel, seq)
    tq_att, tk_att = min(tq_att, seq), min(tk_att, seq)
    topk = min(TOPK_MAX, seq // 4)
    bias, side_a16, side_b16 = select_mask(qit, kw, side_a, side_b,
                                           tq=tq_sel, tk=tk_sel, topk=topk)
    att = masked_attention(qt, kx, vt, bias, bsz=bsz, seq=seq, tq=tq_att, tk=tk_att)
    return att, side_a16, side_b16


def _in_proj_scale(d_att, n_idx, d_gm):
    return jnp.concatenate([
        jnp.full((d_att,), LOG2E * HEAD_DIM ** -0.5, F32),
        jnp.ones((2 * d_att,), F32),
        jnp.full((n_idx,), IDX_DIM ** -0.5, F32),
        jnp.ones((IDX_DIM,), F32),
        jnp.full((IDX_HEADS,), IDX_HEADS ** -0.5, F32),
        jnp.ones((2 * d_gm,), F32)]).reshape(-1, 1)


def kernel(x, norm_mix, w_in, gmlp_v_gain, w_spatial, b_spatial, w_out, norm_ffn,
           w_gate, w_up, w_down, norm_final):
    bsz, seq, d_model = x.shape
    m = bsz * seq
    d_att = ATT_HEADS * HEAD_DIM
    d_gm = GMLP_GROUPS * GMLP_CH
    n_idx = IDX_HEADS * IDX_DIM
    depth = w_in.shape[0]
    tn = 512
    scale = _in_proj_scale(d_att, n_idx, d_gm)
    xf = x.reshape(m, d_model)
    for i in range(depth):
        wt = jnp.swapaxes(w_in[i], 0, 1)
        uv_off = 3 * d_att + n_idx + IDX_DIM + IDX_HEADS
        w_uv = wt[uv_off:].astype(BF16)

        h, kw = rmsnorm_proj_t(xf, norm_mix[i], wt, scale, (3 * d_att + n_idx) // LANES, seq)
        n_qkvi = 3 * d_att + n_idx
        qt, w16 = matmul([h], [(wt, 0)], BF16, n=d_att, col_off=0, w_t=True, scale=scale,
                         side=wt, side_rows=n_qkvi, out_mode="slabs_t", seq=seq, tn=tn,
                         name="proj_q")
        tnb = 1024
        kx = matmul([h], [(w16, 0)], BF16, n=d_att, col_off=d_att // tnb, w_t=True,
                    tn=tnb, name="proj_k")
        vt = matmul([h], [(w16, 0)], BF16, n=d_att, col_off=2 * d_att // tnb, w_t=True,
                    out_mode="slabs_t_ones", seq=seq, tn=tnb, name="proj_v")
        qit = matmul([h], [(w16, 0)], BF16, n=n_idx, col_off=3 * d_att // tnb, w_t=True,
                     post_scale=IDX_DIM ** -0.5, out_mode="slabs_t", seq=seq, tn=tnb,
                     name="proj_qidx")
        gm, wo = gmlp_mixer(h, w_uv, gmlp_v_gain[i], w_spatial[i], b_spatial[i], w_out[i])

        att, wg, wu = dsa_attention(qt, kx, vt, qit, kw, w_gate[i], w_up[i], bsz=bsz, seq=seq)

        xf = matmul([att, gm], [(wo, 0), (wo, 1)], F32, n=d_model, res=xf, tn=1024, name="out_proj")

        h2 = rmsnorm(xf, norm_ffn[i], BF16)
        hid, wd = swiglu_up(h2, wg, wu, w_down[i])
        if i + 1 < depth:
            xf = matmul_k(hid, wd, xf, F32, name="ffn_down")
    y = matmul_k(hid, wd, None, BF16, tn=1024, name="ffn_down")
    out = rmsnorm(xf, norm_final, x.dtype, add=y)
    return out.reshape(bsz, seq, d_model)
```
